```python
import math
import jax
import jax.numpy as jnp
from jax import lax
import numpy as np

D_MODEL = 4096
BATCH = 1
SEQ = 8192
DEPTH = 4
DEC_BATCH = 4
DEC_SEQ = 4096
PAST_LEN = 128

HEAD_DIM = 128
A_WIDTH = 3 * D_MODEL // 8
SSM_WIDTH = 3 * D_MODEL // 8
NA_WIDTH = D_MODEL // 4
MIX_WIDTH = A_WIDTH + SSM_WIDTH + NA_WIDTH
DILATED_GROUPS = ((128, 1), (512, 4), (2048, 16))
A_HEADS = A_WIDTH // HEAD_DIM
A_GROUP_HEADS = A_HEADS // len(DILATED_GROUPS)
A_OUT = A_GROUP_HEADS * HEAD_DIM
ATT_BLOCK = 128
ROT_DIM = HEAD_DIM // 4
ROPE_THETA = 500000.0
SSM_HEAD_DIM = 64
SSM_HEADS = SSM_WIDTH // SSM_HEAD_DIM
SSM_GROUPS = 8
SSM_STATE = 128
SSM_CONV = 5
SSM_CHUNK = 128
SSM_CONV_CH = SSM_WIDTH + 2 * SSM_GROUPS * SSM_STATE
NA_HEADS = NA_WIDTH // HEAD_DIM
GRID_W = 64
NA_ROWS = 8
NA_COLS = 16
NA_QROWS = 2
NA_QCOLS = 16
MEM_TOKENS = 256
MEM_HEADS = 4
MEM_WIDTH = MEM_HEADS * HEAD_DIM
D_FF = 11008
FFN_CONV = 3
EPS = 1e-6
COL_A = 3 * A_WIDTH
COL_Z = COL_A + SSM_WIDTH
COL_XBC = COL_Z + SSM_CONV_CH
COL_DT = COL_XBC + 2 * SSM_HEADS
IN_COLS = COL_DT + 3 * NA_WIDTH
OUT_COLS = A_OUT + SSM_WIDTH + NA_WIDTH

kernel_name = 'hybrid_dilated_ssd_natten_encoder'


def rms_norm(x, g):
    xf = x.astype(jnp.float32)
    y = xf * lax.rsqrt(jnp.mean(xf * xf, axis=-1, keepdims=True) + EPS)
    return (y * g.astype(jnp.float32)).astype(x.dtype)


def depthwise_conv(x, w):
    k = w.shape[0]
    return lax.conv_general_dilated(x, w[:, None, :].astype(x.dtype), window_strides=(1,),
                                    padding=[(k // 2, k // 2)],
                                    dimension_numbers=('NWC', 'WIO', 'NWC'),
                                    feature_group_count=x.shape[-1])


def partial_rotary(t):
    L = t.shape[1]
    half = ROT_DIM // 2
    inv_freq = jnp.exp(-math.log(ROPE_THETA) * jnp.arange(half, dtype=jnp.float32) / half)
    ang = jnp.arange(L, dtype=jnp.float32)[:, None] * inv_freq[None, :]
    cos = jnp.cos(ang)[None, :, None, :]
    sin = jnp.sin(ang)[None, :, None, :]
    tf = t[..., :ROT_DIM].astype(jnp.float32)
    t1, t2 = tf[..., :half], tf[..., half:]
    rot = jnp.concatenate([t1 * cos - t2 * sin, t2 * cos + t1 * sin], axis=-1).astype(t.dtype)
    return jnp.concatenate([rot, t[..., ROT_DIM:]], axis=-1)


def dilated_window_attention(q, k, v, dilation, radius):
    Bsz, L, H, Dh = q.shape
    M = L // dilation

    def to_residue(t):
        return jnp.swapaxes(t.reshape(Bsz, M, dilation, H, Dh), 1, 2)

    qr, kr, vr = to_residue(q), to_residue(k), to_residue(v)
    qb_size = math.gcd(M, ATT_BLOCK)
    nb = M // qb_size
    span = qb_size + 2 * radius
    pad = ((0, 0), (0, 0), (radius, radius), (0, 0), (0, 0))
    kp, vp = jnp.pad(kr, pad), jnp.pad(vr, pad)
    starts = jnp.arange(nb) * qb_size
    idx = starts[:, None] + jnp.arange(span)[None, :]
    kb, vb = kp[:, :, idx], vp[:, :, idx]
    qb = qr.reshape(Bsz, dilation, nb, qb_size, H, Dh)
    s = jnp.einsum('bdnqhe,bdnkhe->bdnhqk', qb, kb).astype(jnp.float32) * (Dh ** -0.5)
    kpos = idx - radius
    qpos = starts[:, None] + jnp.arange(qb_size)[None, :]
    rel = kpos[:, None, :] - qpos[:, :, None]
    valid = (jnp.abs(rel) <= radius) & (kpos[:, None, :] >= 0) & (kpos[:, None, :] < M)
    s = jnp.where(valid[None, None, :, None], s, -jnp.inf)
    lse = jax.nn.logsumexp(s, axis=-1)
    p = jnp.exp(s - lse[..., None])
    o = jnp.einsum('bdnhqk,bdnkhe->bdnqhe', p.astype(v.dtype), vb)
    o = jnp.swapaxes(o.reshape(Bsz, dilation, M, H, Dh), 1, 2).reshape(Bsz, L, H, Dh)
    lse = jnp.moveaxis(lse, 3, 4).reshape(Bsz, dilation, M, H)
    lse = jnp.swapaxes(lse, 1, 2).reshape(Bsz, L, H)
    return o, lse


def ssd_scan(x, dt, a, bm, cm):
    Bsz, L, H, P = x.shape
    G, N = bm.shape[2], bm.shape[3]
    R = H // G
    Q = SSM_CHUNK
    nc = L // Q
    f32 = jnp.float32
    xs = (x.astype(f32) * dt[..., None]).reshape(Bsz, nc, Q, G, R, P)
    da = (dt * a).reshape(Bsz, nc, Q, G, R).transpose(0, 1, 3, 4, 2)
    cum = jnp.cumsum(da, axis=-1)
    bq = bm.astype(f32).reshape(Bsz, nc, Q, G, N)
    cq = cm.astype(f32).reshape(Bsz, nc, Q, G, N)
    tri = jnp.tril(jnp.ones((Q, Q), dtype=bool))
    seg = cum[..., :, None] - cum[..., None, :]
    decay = jnp.exp(jnp.where(tri, seg, -jnp.inf))
    cb = jnp.einsum('bclgn,bcsgn->bcgls', cq, bq)
    y_diag = jnp.einsum('bcgls,bcgrls,bcsgrp->bclgrp', cb, decay, xs)
    states = jnp.einsum('bclgn,bcgrl,bclgrp->bcgrpn', bq, jnp.exp(cum[..., -1:] - cum), xs)
    chunk_decay = jnp.exp(cum[..., -1])

    def carry_state(h, inp):
        s_c, d_c = inp
        return h * d_c[..., None, None] + s_c, h

    h0 = jnp.zeros((Bsz, G, R, P, N), f32)
    _, h_prev = lax.scan(carry_state, h0, (jnp.moveaxis(states, 1, 0), jnp.moveaxis(chunk_decay, 1, 0)))
    h_prev = jnp.moveaxis(h_prev, 0, 1)
    y_off = jnp.einsum('bclgn,bcgrpn,bcgrl->bclgrp', cq, h_prev, jnp.exp(cum))
    return (y_diag + y_off).reshape(Bsz, L, H, P)


def bidirectional_ssd_mixer(z, xbc, dt_raw, lp):
    Bsz, L, _ = z.shape
    f32 = jnp.float32
    xbc = jax.nn.silu(depthwise_conv(xbc, lp['ssm_conv_w']) + lp['ssm_conv_b'].astype(xbc.dtype))
    xs = xbc[..., :SSM_WIDTH].reshape(Bsz, L, SSM_HEADS, SSM_HEAD_DIM)
    bm = xbc[..., SSM_WIDTH:SSM_WIDTH + SSM_GROUPS * SSM_STATE].reshape(Bsz, L, SSM_GROUPS, SSM_STATE)
    cm = xbc[..., SSM_WIDTH + SSM_GROUPS * SSM_STATE:].reshape(Bsz, L, SSM_GROUPS, SSM_STATE)
    dt = jax.nn.softplus(dt_raw.astype(f32).reshape(Bsz, L, 2, SSM_HEADS) + lp['ssm_dt_bias'].astype(f32))
    a = -jnp.exp(lp['ssm_a_log'].astype(f32))
    dt_f, dt_b = dt[:, :, 0], dt[:, :, 1]
    y_f = ssd_scan(xs, dt_f, a[0], bm, cm)
    flip = lambda t: jnp.flip(t, axis=1)
    y_b = flip(ssd_scan(flip(xs), flip(dt_b), a[1], flip(bm), flip(cm)))
    cb_diag = jnp.einsum('blgn,blgn->blg', cm.astype(f32), bm.astype(f32))
    diag = (cb_diag[..., None] * dt_b.reshape(Bsz, L, SSM_GROUPS, -1)).reshape(Bsz, L, SSM_HEADS)
    xf = xs.astype(f32)
    y = y_f + y_b - diag[..., None] * xf + lp['ssm_d'].astype(f32)[:, None] * xf
    y = y.reshape(Bsz, L, SSM_WIDTH) * jax.nn.silu(z.astype(f32))
    return rms_norm(y, lp['ssm_norm']).astype(z.dtype)


def neighbourhood_attention(q, k, v, rpb):
    Bsz, L, H, Dh = q.shape
    rows = L // GRID_W
    kr = min(NA_ROWS, rows)
    kc = NA_COLS
    rb = min(NA_QROWS + kr, rows)
    cb = min(NA_QCOLS + kc, GRID_W)
    nrb, ncb = rows // NA_QROWS, GRID_W // NA_QCOLS
    qrow = jnp.arange(rows).reshape(nrb, NA_QROWS)
    krow = jnp.clip(qrow[:, 0] - kr // 2, 0, rows - rb)[:, None] + jnp.arange(rb)[None, :]
    wr = jnp.clip(qrow - kr // 2, 0, rows - kr)
    row_ok = (krow[:, None, :] >= wr[:, :, None]) & (krow[:, None, :] < wr[:, :, None] + kr)
    row_off = jnp.clip(krow[:, None, :] - qrow[:, :, None] + NA_ROWS - 1, 0, 2 * NA_ROWS - 2)
    qcol = jnp.arange(GRID_W).reshape(ncb, NA_QCOLS)
    kcol = jnp.clip(qcol[:, 0] - kc // 2, 0, GRID_W - cb)[:, None] + jnp.arange(cb)[None, :]
    wc = jnp.clip(qcol - kc // 2, 0, GRID_W - kc)
    col_ok = (kcol[:, None, :] >= wc[:, :, None]) & (kcol[:, None, :] < wc[:, :, None] + kc)
    col_off = jnp.clip(kcol[:, None, :] - qcol[:, :, None] + NA_COLS - 1, 0, 2 * NA_COLS - 2)
    nq, nk = NA_QROWS * NA_QCOLS, rb * cb
    mask = (row_ok[:, None, :, None, :, None] & col_ok[None, :, None, :, None, :]).reshape(nrb, ncb, 1, nq, nk)
    bias = rpb[:, row_off[:, None, :, None, :, None], col_off[None, :, None, :, None, :]]
    bias = bias.reshape(H, nrb, ncb, nq, nk).transpose(1, 2, 0, 3, 4).astype(jnp.float32)
    grid = lambda t: t.reshape(Bsz, rows, GRID_W, H, Dh)
    ri, ci = krow[:, None, :, None], kcol[None, :, None, :]
    kb = grid(k)[:, ri, ci].reshape(Bsz, nrb, ncb, nk, H, Dh)
    vb = grid(v)[:, ri, ci].reshape(Bsz, nrb, ncb, nk, H, Dh)
    qb = grid(q).reshape(Bsz, nrb, NA_QROWS, ncb, NA_QCOLS, H, Dh).transpose(0, 1, 3, 2, 4, 5, 6)
    qb = qb.reshape(Bsz, nrb, ncb, nq, H, Dh)
    s = jnp.einsum('bijqhe,bijkhe->bijhqk', qb, kb).astype(jnp.float32) * (Dh ** -0.5) + bias[None]
    s = jnp.where(mask[None], s, -jnp.inf)
    p = jax.nn.softmax(s, axis=-1)
    o = jnp.einsum('bijhqk,bijkhe->bijqhe', p.astype(v.dtype), vb)
    o = o.reshape(Bsz, nrb, ncb, NA_QROWS, NA_QCOLS, H, Dh).transpose(0, 1, 3, 2, 4, 5, 6)
    return o.reshape(Bsz, L, H * Dh)


def hybrid_mixer(h, lp):
    Bsz, L, _ = h.shape
    proj = h @ lp['w_in']
    a_qkv = proj[..., :COL_A]
    z = proj[..., COL_A:COL_Z]
    xbc = proj[..., COL_Z:COL_XBC]
    dt_raw = proj[..., COL_XBC:COL_DT]
    c_qkv = proj[..., COL_DT:]
    aq, ak, av = [t.reshape(Bsz, L, A_HEADS, HEAD_DIM) for t in jnp.split(a_qkv, 3, axis=-1)]
    aq, ak = partial_rotary(aq), partial_rotary(ak)
    outs, lses = [], []
    for g, (window, dilation) in enumerate(DILATED_GROUPS):
        hs = slice(g * A_GROUP_HEADS, (g + 1) * A_GROUP_HEADS)
        o, lse = dilated_window_attention(aq[:, :, hs], ak[:, :, hs], av[:, :, hs], dilation, window // (2 * dilation))
        outs.append(o)
        lses.append(lse)
    wts = jax.nn.softmax(jnp.stack(lses, axis=0), axis=0)
    o_a = jnp.einsum('gblh,gblhe->blhe', wts, jnp.stack(outs, axis=0).astype(jnp.float32))
    o_a = o_a.reshape(Bsz, L, A_OUT).astype(h.dtype)
    o_b = bidirectional_ssd_mixer(z, xbc, dt_raw, lp)
    cq, ck, cv = [t.reshape(Bsz, L, NA_HEADS, HEAD_DIM) for t in jnp.split(c_qkv, 3, axis=-1)]
    o_c = neighbourhood_attention(cq, ck, cv, lp['na_rpb']).astype(h.dtype)
    return jnp.concatenate([o_a, o_b, o_c], axis=-1) @ lp['w_out']


def memory_cross_attention(h, mem_n, lp):
    Bsz, L, _ = h.shape
    M = mem_n.shape[1]
    q = (h @ lp['mem_wq']).reshape(Bsz, L, MEM_HEADS, HEAD_DIM)
    k = (mem_n @ lp['mem_wk']).reshape(Bsz, M, MEM_HEADS, HEAD_DIM)
    v = (mem_n @ lp['mem_wv']).reshape(Bsz, M, MEM_HEADS, HEAD_DIM)
    s = jnp.einsum('blhe,bmhe->bhlm', q, k).astype(jnp.float32) * (HEAD_DIM ** -0.5)
    p = jax.nn.softmax(s, axis=-1)
    o = jnp.einsum('bhlm,bmhe->blhe', p.astype(v.dtype), v).reshape(Bsz, L, MEM_WIDTH)
    return o @ lp['mem_wo']


def conv_ffn(h, lp):
    up = h @ lp['ffn_w_up']
    gate, val = up[..., :D_FF], up[..., D_FF:]
    gate = depthwise_conv(gate, lp['ffn_conv_w'])
    return (jax.nn.gelu(gate) * val) @ lp['ffn_w_down']


def encoder_layer(x, mem, lp):
    x = x + rms_norm(hybrid_mixer(rms_norm(x, lp['mix_norm_pre']), lp), lp['mix_norm_post'])
    mem_n = rms_norm(mem, lp['mem_norm_kv'])
    x = x + rms_norm(memory_cross_attention(rms_norm(x, lp['mem_norm_pre']), mem_n, lp), lp['mem_norm_post'])
    x = x + rms_norm(conv_ffn(rms_norm(x, lp['ffn_norm_pre']), lp), lp['ffn_norm_post'])
    return x


def setup_inputs(seed: int = 0) -> dict:
    key = jax.random.key(seed)
    k = jax.random.split(key, 28)
    f32 = jnp.float32

    def normal(kk, shape, scale):
        return jax.random.normal(kk, shape, f32) * scale

    def gain(kk, shape):
        return 1.0 + 0.05 * jax.random.normal(kk, shape, f32)

    dt_init = jnp.exp(jax.random.uniform(k[10], (DEPTH, 2, SSM_HEADS), f32, math.log(1e-3), math.log(1e-1)))
    return {
        'x_prompt': normal(k[0], (BATCH, SEQ, D_MODEL), 1.0),
        'x_sample': normal(k[1], (DEC_BATCH, DEC_SEQ, D_MODEL), 1.0),
        'mem_prompt': normal(k[2], (BATCH, MEM_TOKENS, D_MODEL), 1.0),
        'mem_sample': normal(k[3], (DEC_BATCH, MEM_TOKENS, D_MODEL), 1.0),
        'mix_norm_pre': gain(k[4], (DEPTH, D_MODEL)),
        'mix_norm_post': gain(k[5], (DEPTH, D_MODEL)),
        'w_in': normal(k[6], (DEPTH, D_MODEL, IN_COLS), D_MODEL ** -0.5),
        'ssm_conv_w': normal(k[7], (DEPTH, SSM_CONV, SSM_CONV_CH), SSM_CONV ** -0.5),
        'ssm_conv_b': normal(k[8], (DEPTH, SSM_CONV_CH), 0.02),
        'ssm_a_log': jnp.log(jax.random.uniform(k[9], (DEPTH, 2, SSM_HEADS), f32, 1.0, 16.0)),
        'ssm_dt_bias': dt_init + jnp.log(-jnp.expm1(-dt_init)),
        'ssm_d': gain(k[11], (DEPTH, SSM_HEADS)),
        'ssm_norm': gain(k[12], (DEPTH, SSM_WIDTH)),
        'na_rpb': normal(k[13], (DEPTH, NA_HEADS, 2 * NA_ROWS - 1, 2 * NA_COLS - 1), 0.1),
        'w_out': normal(k[14], (DEPTH, OUT_COLS, D_MODEL), OUT_COLS ** -0.5),
        'mem_norm_pre': gain(k[15], (DEPTH, D_MODEL)),
        'mem_norm_post': gain(k[16], (DEPTH, D_MODEL)),
        'mem_norm_kv': gain(k[17], (DEPTH, D_MODEL)),
        'mem_wq': normal(k[18], (DEPTH, D_MODEL, MEM_WIDTH), D_MODEL ** -0.5),
        'mem_wk': normal(k[19], (DEPTH, D_MODEL, MEM_WIDTH), D_MODEL ** -0.5),
        'mem_wv': normal(k[20], (DEPTH, D_MODEL, MEM_WIDTH), D_MODEL ** -0.5),
        'mem_wo': normal(k[21], (DEPTH, MEM_WIDTH, D_MODEL), MEM_WIDTH ** -0.5),
        'ffn_norm_pre': gain(k[22], (DEPTH, D_MODEL)),
        'ffn_norm_post': gain(k[23], (DEPTH, D_MODEL)),
        'ffn_w_up': normal(k[24], (DEPTH, D_MODEL, 2 * D_FF), D_MODEL ** -0.5),
        'ffn_conv_w': normal(k[25], (DEPTH, FFN_CONV, D_FF), FFN_CONV ** -0.5),
        'ffn_w_down': normal(k[26], (DEPTH, D_FF, D_MODEL), D_FF ** -0.5),
    }


def reference(x_prompt, x_sample, mem_prompt, mem_sample, mix_norm_pre, mix_norm_post, w_in,
              ssm_conv_w, ssm_conv_b, ssm_a_log, ssm_dt_bias, ssm_d, ssm_norm, na_rpb, w_out,
              mem_norm_pre, mem_norm_post, mem_norm_kv, mem_wq, mem_wk, mem_wv, mem_wo,
              ffn_norm_pre, ffn_norm_post, ffn_w_up, ffn_conv_w, ffn_w_down):
    y_prompt, y_sample = x_prompt, x_sample
    for l in range(DEPTH):
        lp = {
            'mix_norm_pre': mix_norm_pre[l], 'mix_norm_post': mix_norm_post[l], 'w_in': w_in[l],
            'ssm_conv_w': ssm_conv_w[l], 'ssm_conv_b': ssm_conv_b[l], 'ssm_a_log': ssm_a_log[l],
            'ssm_dt_bias': ssm_dt_bias[l], 'ssm_d': ssm_d[l], 'ssm_norm': ssm_norm[l],
            'na_rpb': na_rpb[l], 'w_out': w_out[l],
            'mem_norm_pre': mem_norm_pre[l], 'mem_norm_post': mem_norm_post[l], 'mem_norm_kv': mem_norm_kv[l],
            'mem_wq': mem_wq[l], 'mem_wk': mem_wk[l], 'mem_wv': mem_wv[l], 'mem_wo': mem_wo[l],
            'ffn_norm_pre': ffn_norm_pre[l], 'ffn_norm_post': ffn_norm_post[l],
            'ffn_w_up': ffn_w_up[l], 'ffn_conv_w': ffn_conv_w[l], 'ffn_w_down': ffn_w_down[l],
        }
        y_prompt = encoder_layer(y_prompt, mem_prompt, lp)
        y_sample = encoder_layer(y_sample, mem_sample, lp)
    return (y_prompt, y_sample)
```

```python
import functools
import math

import jax
import jax.numpy as jnp
import numpy as np
from jax import lax
from jax.experimental import pallas as pl
from jax.experimental.pallas import tpu as pltpu

F32 = jnp.float32
BF16 = jnp.bfloat16

D_MODEL = 4096
DEPTH = 4
HEAD_DIM = 128
A_WIDTH = 1536
SSM_WIDTH = 1536
NA_WIDTH = 1024
DILATED_GROUPS = ((128, 1), (512, 4), (2048, 16))
A_GROUP_HEADS = 4
A_OUT = A_GROUP_HEADS * HEAD_DIM
ROT_DIM = 32
ROPE_THETA = 500000.0
SSM_HEAD_DIM = 64
SSM_HEADS = 24
SSM_GROUPS = 8
SSM_GROUP_HEADS = SSM_HEADS // SSM_GROUPS
SSM_STATE = 128
SSM_CONV = 5
SSM_CHUNK = 128
NA_HEADS = 8
GRID_W = 64
NA_ROWS = 8
NA_COLS = 16
NA_KEY_ROWS = 10
MEM_TOKENS = 256
MEM_HEADS = 4
MEM_WIDTH = MEM_HEADS * HEAD_DIM
D_FF = 11008
D_FF_PAD = 11264
EPS = 1e-6

COL_AQ, COL_AK, COL_AV, COL_Z, COL_X, COL_DT = 0, 1536, 3072, 4608, 6144, 7680
COL_CQ, COL_CK, COL_CV, COL_B, COL_C = 8192, 9216, 10240, 11264, 12288
IN_COLS_PAD = 13312
DT_LANES = 128

VMEM_LIMIT = 56 * 1024 * 1024
NEG_BIG = -1e30


def _cparams(*sem):
    return pltpu.CompilerParams(dimension_semantics=sem, vmem_limit_bytes=VMEM_LIMIT)


def _rms(x, g):
    ms = jnp.mean(x * x, axis=-1, keepdims=True)
    return x * lax.rsqrt(ms + EPS) * g


def _rmsnorm_body(x_ref, g_ref, o_ref):
    o_ref[...] = _rms(x_ref[...], g_ref[...]).astype(o_ref.dtype)


def rmsnorm(x, g, tm=256):
    T, D = x.shape
    return pl.pallas_call(
        _rmsnorm_body,
        grid=(T // tm,),
        in_specs=[pl.BlockSpec((tm, D), lambda i: (i, 0)), pl.BlockSpec((1, D), lambda i: (0, 0))],
        out_specs=pl.BlockSpec((tm, D), lambda i: (i, 0)),
        out_shape=jax.ShapeDtypeStruct((T, D), BF16),
        compiler_params=_cparams("parallel"),
        name="rmsnorm",
    )(x, g.reshape(1, D))


def _mm_body(x_ref, w_ref, o_ref):
    o_ref[...] = jnp.dot(x_ref[...], w_ref[...], preferred_element_type=F32).astype(o_ref.dtype)


def matmul(x, w, tm, tn, name):
    T, K = x.shape
    N = w.shape[1]
    return pl.pallas_call(
        _mm_body,
        grid=(T // tm, N // tn),
        in_specs=[pl.BlockSpec((tm, K), lambda i, j: (i, 0)), pl.BlockSpec((K, tn), lambda i, j: (0, j))],
        out_specs=pl.BlockSpec((tm, tn), lambda i, j: (i, j)),
        out_shape=jax.ShapeDtypeStruct((T, N), BF16),
        compiler_params=_cparams("parallel", "arbitrary"),
        name=name,
    )(x, w)


def _residual_epilogue(y, x, g_post, g_next, xo_ref, xn_ref):
    x_new = x + _rms(y, g_post)
    xo_ref[...] = x_new
    xn_ref[...] = _rms(x_new, g_next).astype(xn_ref.dtype)


def _mm_res_body(h_ref, w_ref, x_ref, gp_ref, gn_ref, xo_ref, xn_ref):
    k = pl.program_id(1)
    part = jnp.dot(h_ref[...], w_ref[...], preferred_element_type=F32)

    @pl.when(k == 0)
    def _():
        xo_ref[...] = part

    @pl.when(k > 0)
    def _():
        xo_ref[...] += part

    @pl.when(k == pl.num_programs(1) - 1)
    def _():
        _residual_epilogue(xo_ref[...], x_ref[...], gp_ref[...], gn_ref[...], xo_ref, xn_ref)


def matmul_residual(h, w, x, g_post, g_next, tm, tk, name):
    T, K = h.shape
    D = w.shape[1]
    row = lambda i, k: (i, 0)
    return pl.pallas_call(
        _mm_res_body,
        grid=(T // tm, K // tk),
        in_specs=[
            pl.BlockSpec((tm, tk), lambda i, k: (i, k)),
            pl.BlockSpec((tk, D), lambda i, k: (k, 0)),
            pl.BlockSpec((tm, D), row),
            pl.BlockSpec((1, D), lambda i, k: (0, 0)),
            pl.BlockSpec((1, D), lambda i, k: (0, 0)),
        ],
        out_specs=[pl.BlockSpec((tm, D), row), pl.BlockSpec((tm, D), row)],
        out_shape=[jax.ShapeDtypeStruct((T, D), F32), jax.ShapeDtypeStruct((T, D), BF16)],
        compiler_params=_cparams("parallel", "arbitrary"),
        name=name,
    )(h, w, x, g_post.reshape(1, D), g_next.reshape(1, D))


def rotary_table(L):
    half = ROT_DIM // 2
    inv_freq = jnp.exp(-math.log(ROPE_THETA) * jnp.arange(half, dtype=F32) / half)
    ang = jnp.arange(L, dtype=F32)[:, None] * inv_freq[None, :]
    cos, sin = jnp.cos(ang), jnp.sin(ang)
    ones = jnp.ones((L, HEAD_DIM - ROT_DIM), F32)
    zeros = jnp.zeros((L, HEAD_DIM - ROT_DIM), F32)
    return jnp.concatenate([cos, cos, ones, -sin, sin, zeros], axis=1)


def _rope(t, tab):
    lane = lax.broadcasted_iota(jnp.int32, t.shape, 1)
    partner = jnp.where(lane < ROT_DIM // 2, pltpu.roll(t, HEAD_DIM - ROT_DIM // 2, 1), pltpu.roll(t, ROT_DIM // 2, 1))
    return t * tab[:, :HEAD_DIM] + partner * tab[:, HEAD_DIM:]


def _dilated_body(q_ref, kp_ref, kc_ref, kn_ref, vp_ref, vc_ref, vn_ref, tp_ref, tc_ref, tn_ref,
                  o_ref, lse_ref, *, radius, m_len):
    m = pl.program_id(2)
    tq = q_ref.shape[0]
    row = lax.broadcasted_iota(jnp.int32, (tq, 3 * tq), 0)
    col = lax.broadcasted_iota(jnp.int32, (tq, 3 * tq), 1)
    kpos = (m - 1) * tq + col
    rel = col - tq - row
    valid = (jnp.abs(rel) <= radius) & (kpos >= 0) & (kpos < m_len)
    scale = HEAD_DIM ** -0.5
    nt = (((1,), (1,)), ((), ()))
    for h in range(A_GROUP_HEADS):
        hs = slice(h * HEAD_DIM, (h + 1) * HEAD_DIM)
        q = _rope(q_ref[:, hs].astype(F32), tc_ref[...]).astype(BF16)
        parts = []
        for k_ref, t_ref in ((kp_ref, tp_ref), (kc_ref, tc_ref), (kn_ref, tn_ref)):
            kk = _rope(k_ref[:, hs].astype(F32), t_ref[...]).astype(BF16)
            parts.append(lax.dot_general(q, kk, nt, preferred_element_type=F32))
        s = jnp.concatenate(parts, axis=1) * scale
        s = jnp.where(valid, s, NEG_BIG)
        mx = jnp.max(s, axis=-1, keepdims=True)
        p = jnp.exp(s - mx)
        l = jnp.sum(p, axis=-1, keepdims=True)
        pb = p.astype(BF16)
        o = jnp.dot(pb[:, :tq], vp_ref[:, hs], preferred_element_type=F32)
        o += jnp.dot(pb[:, tq:2 * tq], vc_ref[:, hs], preferred_element_type=F32)
        o += jnp.dot(pb[:, 2 * tq:], vn_ref[:, hs], preferred_element_type=F32)
        o_ref[:, hs] = o / l
        lse_ref[:, hs] = jnp.broadcast_to(mx + jnp.log(l), (tq, HEAD_DIM))


def dilated_attention(proj, tab, g, dil, radius):
    B, L, C = proj.shape
    M = L // dil
    tq = 128
    nmb = M // tq
    ncb = C // A_OUT
    pv = proj.reshape(B, M, dil * C)
    tv = tab.reshape(M, dil * 2 * HEAD_DIM)

    def blk(off, col0):
        return pl.BlockSpec((None, tq, A_OUT),
                            lambda b, r, m: (b, jnp.clip(m + off, 0, nmb - 1), r * ncb + col0 + g))

    def tblk(off):
        return pl.BlockSpec((tq, 2 * HEAD_DIM), lambda b, r, m: (jnp.clip(m + off, 0, nmb - 1), r))

    cq, ck, cv = COL_AQ // A_OUT, COL_AK // A_OUT, COL_AV // A_OUT
    out_spec = pl.BlockSpec((None, tq, A_OUT), lambda b, r, m: (b, m, r))
    o, lse = pl.pallas_call(
        functools.partial(_dilated_body, radius=radius, m_len=M),
        grid=(B, dil, nmb),
        in_specs=[blk(0, cq), blk(-1, ck), blk(0, ck), blk(1, ck), blk(-1, cv), blk(0, cv), blk(1, cv),
                  tblk(-1), tblk(0), tblk(1)],
        out_specs=[out_spec, out_spec],
        out_shape=[jax.ShapeDtypeStruct((B, M, dil * A_OUT), F32)] * 2,
        compiler_params=_cparams("parallel", "parallel", "arbitrary"),
        name=f"dilated_attn_d{dil}",
    )(pv, pv, pv, pv, pv, pv, pv, tv, tv, tv)
    return o.reshape(B * L, A_OUT), lse.reshape(B * L, A_OUT)


def _combine_body(o0, o1, o2, l0, l1, l2, out_ref):
    a, b, c = l0[...], l1[...], l2[...]
    mx = jnp.maximum(jnp.maximum(a, b), c)
    ea, eb, ec = jnp.exp(a - mx), jnp.exp(b - mx), jnp.exp(c - mx)
    den = ea + eb + ec
    out_ref[...] = ((ea * o0[...] + eb * o1[...] + ec * o2[...]) / den).astype(out_ref.dtype)


def combine_groups(outs, lses, tm=512):
    T, W = outs[0].shape
    spec = pl.BlockSpec((tm, W), lambda i: (i, 0))
    return pl.pallas_call(
        _combine_body,
        grid=(T // tm,),
        in_specs=[spec] * 6,
        out_specs=spec,
        out_shape=jax.ShapeDtypeStruct((T, W), BF16),
        compiler_params=_cparams("parallel"),
        name="dilated_combine",
    )(*outs, *lses)


def _conv_silu_body(x_ref, xp_ref, xn_ref, w_ref, b_ref, o_ref, ext_ref, *, halo):
    i = pl.program_id(1)
    tl = x_ref.shape[0]
    first = i == 0
    last = i == pl.num_programs(1) - 1
    ext_ref[0:halo, :] = jnp.where(first, 0.0, xp_ref[...].astype(F32))
    ext_ref[halo:halo + tl, :] = x_ref[...].astype(F32)
    ext_ref[halo + tl:, :] = jnp.where(last, 0.0, xn_ref[...].astype(F32))
    acc = jnp.broadcast_to(b_ref[...], (tl, b_ref.shape[1]))
    pad = SSM_CONV // 2
    for k in range(SSM_CONV):
        acc = acc + w_ref[k:k + 1, :] * ext_ref[halo - pad + k:halo - pad + k + tl, :]
    o_ref[...] = (acc / (1.0 + jnp.exp(-acc))).astype(o_ref.dtype)


def conv_silu(proj, col0, w, b, tl=512, tc=512):
    B, L, _ = proj.shape
    C = w.shape[1]
    halo = 16
    cb0 = col0 // tc
    hb = tl // halo
    nhb = L // halo
    return pl.pallas_call(
        functools.partial(_conv_silu_body, halo=halo),
        grid=(B, L // tl, C // tc),
        in_specs=[
            pl.BlockSpec((None, tl, tc), lambda b, i, j: (b, i, cb0 + j)),
            pl.BlockSpec((None, halo, tc), lambda b, i, j: (b, jnp.maximum(i * hb - 1, 0), cb0 + j)),
            pl.BlockSpec((None, halo, tc), lambda b, i, j: (b, jnp.minimum((i + 1) * hb, nhb - 1), cb0 + j)),
            pl.BlockSpec((SSM_CONV, tc), lambda b, i, j: (0, j)),
            pl.BlockSpec((1, tc), lambda b, i, j: (0, j)),
        ],
        out_specs=pl.BlockSpec((None, tl, tc), lambda b, i, j: (b, i, j)),
        out_shape=jax.ShapeDtypeStruct((B, L, C), BF16),
        scratch_shapes=[pltpu.VMEM((tl + 2 * halo, tc), F32)],
        compiler_params=_cparams("parallel", "parallel", "arbitrary"),
        name="ssm_conv_silu",
    )(proj, proj, proj, w, b.reshape(1, C))


def _split3(a):
    hi = a.astype(BF16)
    r1 = a - hi.astype(F32)
    mid = r1.astype(BF16)
    lo = (r1 - mid.astype(F32)).astype(BF16)
    return jnp.concatenate([hi, mid, lo], axis=1)


def _ssd_chunk_terms(dt_ref, bias_ref, alog_ref):
    Q = SSM_CHUNK
    lane = lax.broadcasted_iota(jnp.int32, (Q, DT_LANES), 1)
    row = lax.broadcasted_iota(jnp.int32, (Q, DT_LANES), 0)
    v = dt_ref[...].astype(F32) + bias_ref[...]
    dt = jnp.maximum(v, 0.0) + jnp.log1p(jnp.exp(-jnp.abs(v)))
    a_row = -jnp.exp(alog_ref[...])
    da = jnp.where(lane < 2 * SSM_HEADS, dt * a_row, 0.0)
    pre, suf = da, da
    s = 1
    while s < Q:
        pre = pre + jnp.where(row >= s, pltpu.roll(pre, s, 0), 0.0)
        suf = suf + jnp.where(row < Q - s, pltpu.roll(suf, Q - s, 0), 0.0)
        s *= 2
    is_fwd = lane < SSM_HEADS
    cs = jnp.where(is_fwd, pre, suf)
    tot = jnp.where(is_fwd[:1], cs[Q - 1:Q, :], cs[0:1, :])
    e = jnp.exp(cs)
    w = jnp.exp(tot - cs) * dt
    return dt, cs, e, w


def _ssd_fwd_body(x_ref, bc_ref, dt_ref, bias_ref, alog_ref, d_ref, sel_ref, exp_ref, y_ref, h_ref):
    Q, G, R, P, N = SSM_CHUNK, SSM_GROUPS, SSM_GROUP_HEADS, SSM_HEAD_DIM, SSM_STATE
    GP = R * P

    @pl.when(pl.program_id(1) == 0)
    def _():
        h_ref[...] = jnp.zeros_like(h_ref)

    dt, cs, e, w = _ssd_chunk_terms(dt_ref, bias_ref, alog_ref)
    cs_t = cs.T
    dt_t = dt.T
    colb = jnp.dot(_split3(cs), sel_ref[...], preferred_element_type=F32)
    e_x = jnp.dot(_split3(e), exp_ref[:, :SSM_WIDTH], preferred_element_type=F32)
    w_x = jnp.dot(_split3(w), exp_ref[:, :SSM_WIDTH], preferred_element_type=F32)
    x = x_ref[...]
    xf = x.astype(F32)
    xw = (xf * w_x).astype(BF16)
    tri = lax.broadcasted_iota(jnp.int32, (Q, Q), 0) >= lax.broadcasted_iota(jnp.int32, (Q, Q), 1)
    nt = (((1,), (1,)), ((), ()))
    ys, y_offs = [], []
    for g in range(G):
        bg = bc_ref[:, g * N:(g + 1) * N]
        cg = bc_ref[:, G * N + g * N:G * N + (g + 1) * N]
        cb = lax.dot_general(cg, bg, nt, preferred_element_type=F32)
        for r in range(R):
            h = g * R + r
            hb = SSM_HEADS + h
            arg = jnp.where(tri, colb[:, h * Q:(h + 1) * Q] - cs_t[h:h + 1, :],
                            colb[:, hb * Q:(hb + 1) * Q] - cs_t[hb:hb + 1, :])
            dtrow = jnp.where(tri, dt_t[h:h + 1, :], dt_t[hb:hb + 1, :])
            wm = (cb * jnp.exp(arg) * dtrow).astype(BF16)
            ys.append(jnp.dot(wm, x[:, h * P:(h + 1) * P], preferred_element_type=F32))
        gs = slice(g * GP, (g + 1) * GP)
        hg = h_ref[g]
        y_offs.append(jnp.dot(cg, hg.astype(BF16), preferred_element_type=F32))
        bg_t = bg.astype(F32).T.astype(BF16)
        h_ref[g] = hg * e_x[Q - 1:Q, gs] + jnp.dot(bg_t, xw[:, gs], preferred_element_type=F32)
    y_ref[...] = jnp.concatenate(ys, axis=1) + jnp.concatenate(y_offs, axis=1) * e_x + d_ref[...] * xf


def _ssd_bwd_body(x_ref, bc_ref, dt_ref, bias_ref, alog_ref, exp_ref, yf_ref, z_ref, gain_ref, o_ref, h_ref):
    Q, G, R, P, N = SSM_CHUNK, SSM_GROUPS, SSM_GROUP_HEADS, SSM_HEAD_DIM, SSM_STATE
    GP = R * P

    @pl.when(pl.program_id(1) == 0)
    def _():
        h_ref[...] = jnp.zeros_like(h_ref)

    _, _, e, w = _ssd_chunk_terms(dt_ref, bias_ref, alog_ref)
    e_x = jnp.dot(_split3(e), exp_ref[:, SSM_WIDTH:], preferred_element_type=F32)
    w_x = jnp.dot(_split3(w), exp_ref[:, SSM_WIDTH:], preferred_element_type=F32)
    xw = (x_ref[...].astype(F32) * w_x).astype(BF16)
    ys = []
    for g in range(G):
        bg = bc_ref[:, g * N:(g + 1) * N]
        cg = bc_ref[:, G * N + g * N:G * N + (g + 1) * N]
        gs = slice(g * GP, (g + 1) * GP)
        hg = h_ref[g]
        ys.append(jnp.dot(cg, hg.astype(BF16), preferred_element_type=F32))
        bg_t = bg.astype(F32).T.astype(BF16)
        h_ref[g] = hg * e_x[0:1, gs] + jnp.dot(bg_t, xw[:, gs], preferred_element_type=F32)
    y = yf_ref[...] + jnp.concatenate(ys, axis=1) * e_x
    z = z_ref[...].astype(F32)
    y = y * (z / (1.0 + jnp.exp(-z)))
    o_ref[...] = _rms(y, gain_ref[...]).astype(o_ref.dtype)


def _ssd_constants():
    sel = np.zeros((DT_LANES, 2 * SSM_HEADS * SSM_CHUNK), np.float32)
    for h in range(2 * SSM_HEADS):
        sel[h, h * SSM_CHUNK:(h + 1) * SSM_CHUNK] = 1.0
    ex = np.zeros((DT_LANES, 2 * SSM_WIDTH), np.float32)
    for h in range(2 * SSM_HEADS):
        ex[h, h * SSM_HEAD_DIM:(h + 1) * SSM_HEAD_DIM] = 1.0
    return jnp.asarray(np.tile(sel, (3, 1)), BF16), jnp.asarray(np.tile(ex, (3, 1)), BF16)


def ssd_mixer(proj, xs, bc, dt_bias, a_log, d_row, gain):
    B, L, _ = proj.shape
    Q = SSM_CHUNK
    nc = L // Q
    sel, ex = _ssd_constants()
    full = lambda shape: pl.BlockSpec(shape, lambda b, c: (0,) * len(shape))
    state = pltpu.VMEM((SSM_GROUPS, SSM_STATE, SSM_GROUP_HEADS * SSM_HEAD_DIM), F32)

    def chunk(width, col_block, rev):
        if rev:
            return pl.BlockSpec((None, Q, width), lambda b, c: (b, nc - 1 - c, col_block))
        return pl.BlockSpec((None, Q, width), lambda b, c: (b, c, col_block))

    y_f = pl.pallas_call(
        _ssd_fwd_body,
        grid=(B, nc),
        in_specs=[chunk(SSM_WIDTH, 0, False), chunk(2 * SSM_GROUPS * SSM_STATE, 0, False),
                  chunk(DT_LANES, COL_DT // DT_LANES, False),
                  full((1, DT_LANES)), full((1, DT_LANES)), full((1, SSM_WIDTH)),
                  full(sel.shape), full(ex.shape)],
        out_specs=chunk(SSM_WIDTH, 0, False),
        out_shape=jax.ShapeDtypeStruct((B, L, SSM_WIDTH), F32),
        scratch_shapes=[state],
        compiler_params=_cparams("parallel", "arbitrary"),
        name="ssd_fwd",
    )(xs, bc, proj, dt_bias, a_log, d_row, sel, ex)
    o_b = pl.pallas_call(
        _ssd_bwd_body,
        grid=(B, nc),
        in_specs=[chunk(SSM_WIDTH, 0, True), chunk(2 * SSM_GROUPS * SSM_STATE, 0, True),
                  chunk(DT_LANES, COL_DT // DT_LANES, True),
                  full((1, DT_LANES)), full((1, DT_LANES)), full(ex.shape),
                  chunk(SSM_WIDTH, 0, True), chunk(SSM_WIDTH, COL_Z // SSM_WIDTH, True), full((1, SSM_WIDTH))],
        out_specs=chunk(SSM_WIDTH, 0, True),
        out_shape=jax.ShapeDtypeStruct((B, L, SSM_WIDTH), BF16),
        scratch_shapes=[state],
        compiler_params=_cparams("parallel", "arbitrary"),
        name="ssd_bwd",
    )(xs, bc, proj, dt_bias, a_log, ex, y_f, proj, gain)
    return o_b.reshape(B * L, SSM_WIDTH)


def na_bias_table(rpb, rows):
    nrb = rows // 2
    ii = jnp.array([0, 1, 2, nrb - 2, nrb - 1], jnp.int32)
    qrow = ii[:, None] * 2 + jnp.arange(2)[None, :]
    kstart = jnp.clip(ii * 2 - NA_ROWS // 2, 0, rows - NA_KEY_ROWS)
    krow = kstart[:, None] + jnp.arange(NA_KEY_ROWS)[None, :]
    wr = jnp.clip(qrow - NA_ROWS // 2, 0, rows - NA_ROWS)
    row_ok = (krow[:, None, :] >= wr[:, :, None]) & (krow[:, None, :] < wr[:, :, None] + NA_ROWS)
    row_off = jnp.clip(krow[:, None, :] - qrow[:, :, None] + NA_ROWS - 1, 0, 2 * NA_ROWS - 2)
    qcol = jnp.arange(GRID_W)
    wc = jnp.clip(qcol - NA_COLS // 2, 0, GRID_W - NA_COLS)
    col_ok = (qcol[None, :] >= wc[:, None]) & (qcol[None, :] < wc[:, None] + NA_COLS)
    col_off = jnp.clip(qcol[None, :] - qcol[:, None] + NA_COLS - 1, 0, 2 * NA_COLS - 2)
    bias = rpb[:, row_off[:, :, None, :, None], col_off[None, None, :, None, :]]
    mask = row_ok[:, :, None, :, None] & col_ok[None, None, :, None, :]
    tab = jnp.where(mask[None], bias.astype(F32), NEG_BIG)
    H = rpb.shape[0]
    return tab.transpose(1, 0, 2, 3, 4, 5).reshape(5, H, 2 * GRID_W, NA_KEY_ROWS * GRID_W)


def _na_body(q_ref, k0, k1, k2, k3, k4, v0, v1, v2, v3, v4, bias_ref, o_ref):
    scale = HEAD_DIM ** -0.5
    nt = (((1,), (1,)), ((), ()))
    k_refs = (k0, k1, k2, k3, k4)
    v_refs = (v0, v1, v2, v3, v4)
    tq = q_ref.shape[0]
    for h in range(NA_HEADS):
        hs = slice(h * HEAD_DIM, (h + 1) * HEAD_DIM)
        q = q_ref[:, hs]
        s = jnp.concatenate([lax.dot_general(q, kr[:, hs], nt, preferred_element_type=F32) for kr in k_refs], axis=1)
        s = s * scale + bias_ref[h]
        mx = jnp.max(s, axis=-1, keepdims=True)
        p = jnp.exp(s - mx)
        l = jnp.sum(p, axis=-1, keepdims=True)
        pb = p.astype(BF16)
        o = jnp.dot(pb[:, :tq], v_refs[0][:, hs], preferred_element_type=F32)
        for j in range(1, 5):
            o += jnp.dot(pb[:, j * tq:(j + 1) * tq], v_refs[j][:, hs], preferred_element_type=F32)
        o_ref[:, hs] = (o / l).astype(o_ref.dtype)


def neighbourhood_attention(proj, bias_tab):
    B, L, _ = proj.shape
    tq = 2 * GRID_W
    nrb = L // tq
    W = NA_WIDTH

    def pattern(i):
        return jnp.where(i < 2, i, jnp.where(i >= nrb - 2, i - (nrb - 5), 2))

    def kv(j, col):
        return pl.BlockSpec((None, tq, W), lambda b, i: (b, jnp.clip(i - 2, 0, nrb - 5) + j, col))

    cq, ck, cv = COL_CQ // W, COL_CK // W, COL_CV // W
    o = pl.pallas_call(
        _na_body,
        grid=(B, nrb),
        in_specs=[pl.BlockSpec((None, tq, W), lambda b, i: (b, i, cq))]
        + [kv(j, ck) for j in range(5)] + [kv(j, cv) for j in range(5)]
        + [pl.BlockSpec((None, NA_HEADS, tq, 5 * tq), lambda b, i: (pattern(i), 0, 0, 0))],
        out_specs=pl.BlockSpec((None, tq, W), lambda b, i: (b, i, 0)),
        out_shape=jax.ShapeDtypeStruct((B, L, W), BF16),
        compiler_params=_cparams("parallel", "arbitrary"),
        name="neighbourhood_attn",
    )(*([proj] * 11), bias_tab)
    return o.reshape(B * L, W)


def _mem_attn_body(xn_ref, x_ref, kv_ref, wq_ref, wo_ref, gp_ref, gn_ref, xo_ref, xnn_ref):
    scale = HEAD_DIM ** -0.5
    nt = (((1,), (1,)), ((), ()))
    q = jnp.dot(xn_ref[...], wq_ref[...], preferred_element_type=F32).astype(BF16)
    outs = []
    for h in range(MEM_HEADS):
        hs = slice(h * HEAD_DIM, (h + 1) * HEAD_DIM)
        vs = slice(MEM_WIDTH + h * HEAD_DIM, MEM_WIDTH + (h + 1) * HEAD_DIM)
        s = lax.dot_general(q[:, hs], kv_ref[:, hs], nt, preferred_element_type=F32) * scale
        mx = jnp.max(s, axis=-1, keepdims=True)
        p = jnp.exp(s - mx)
        l = jnp.sum(p, axis=-1, keepdims=True)
        o = jnp.dot(p.astype(BF16), kv_ref[:, vs], preferred_element_type=F32) / l
        outs.append(o.astype(BF16))
    y = jnp.dot(jnp.concatenate(outs, axis=1), wo_ref[...], preferred_element_type=F32)
    _residual_epilogue(y, x_ref[...], gp_ref[...], gn_ref[...], xo_ref, xnn_ref)


def memory_attention(xn, x, kv, wq, wo, g_post, g_next, tm=256):
    B, L, D = x.shape
    tok = pl.BlockSpec((None, tm, D), lambda b, i: (b, i, 0))
    full = lambda shape: pl.BlockSpec(shape, lambda b, i: (0,) * len(shape))
    return pl.pallas_call(
        _mem_attn_body,
        grid=(B, L // tm),
        in_specs=[tok, tok, pl.BlockSpec((None, MEM_TOKENS, 2 * MEM_WIDTH), lambda b, i: (b, 0, 0)),
                  full((D, MEM_WIDTH)), full((MEM_WIDTH, D)), full((1, D)), full((1, D))],
        out_specs=[tok, tok],
        out_shape=[jax.ShapeDtypeStruct((B, L, D), F32), jax.ShapeDtypeStruct((B, L, D), BF16)],
        compiler_params=_cparams("parallel", "arbitrary"),
        name="memory_attn",
    )(xn, x, kv, wq, wo, g_post.reshape(1, D), g_next.reshape(1, D))


def _ffn_down_body(g_ref, gp_ref, gn_ref, v_ref, cw_ref, w_ref, x_ref, gpost_ref, gnext_ref,
                   xo_ref, xn_ref, ext_ref, *, halo, tiles_per_seq):
    i = pl.program_id(0)
    k = pl.program_id(1)
    tm = g_ref.shape[0]
    pos = lax.rem(i, tiles_per_seq)
    ext_ref[0:halo, :] = jnp.where(pos == 0, 0.0, gp_ref[...].astype(F32))
    ext_ref[halo:halo + tm, :] = g_ref[...].astype(F32)
    ext_ref[halo + tm:, :] = jnp.where(pos == tiles_per_seq - 1, 0.0, gn_ref[...].astype(F32))
    conv = (cw_ref[0:1, :] * ext_ref[halo - 1:halo - 1 + tm, :]
            + cw_ref[1:2, :] * ext_ref[halo:halo + tm, :]
            + cw_ref[2:3, :] * ext_ref[halo + 1:halo + 1 + tm, :])
    cdf = 0.5 * (1.0 + jnp.tanh(math.sqrt(2.0 / math.pi) * (conv + 0.044715 * (conv * conv * conv))))
    act = (conv * cdf * v_ref[...].astype(F32)).astype(BF16)
    part = jnp.dot(act, w_ref[...], preferred_element_type=F32)

    @pl.when(k == 0)
    def _():
        xo_ref[...] = part

    @pl.when(k > 0)
    def _():
        xo_ref[...] += part

    @pl.when(k == pl.num_programs(1) - 1)
    def _():
        _residual_epilogue(xo_ref[...], x_ref[...], gpost_ref[...], gnext_ref[...], xo_ref, xn_ref)


def ffn_down(up, conv_w, w_down, x, g_post, g_next, seq_len, tm=256, tk=1024):
    T, F2 = up.shape
    F = F2 // 2
    D = w_down.shape[1]
    halo = 16
    nk = F // tk
    hb = tm // halo
    nhb = T // halo
    row = lambda i, k: (i, 0)
    return pl.pallas_call(
        functools.partial(_ffn_down_body, halo=halo, tiles_per_seq=seq_len // tm),
        grid=(T // tm, nk),
        in_specs=[
            pl.BlockSpec((tm, tk), lambda i, k: (i, k)),
            pl.BlockSpec((halo, tk), lambda i, k: (jnp.maximum(i * hb - 1, 0), k)),
            pl.BlockSpec((halo, tk), lambda i, k: (jnp.minimum((i + 1) * hb, nhb - 1), k)),
            pl.BlockSpec((tm, tk), lambda i, k: (i, nk + k)),
            pl.BlockSpec((3, tk), lambda i, k: (0, k)),
            pl.BlockSpec((tk, D), lambda i, k: (k, 0)),
            pl.BlockSpec((tm, D), row),
            pl.BlockSpec((1, D), lambda i, k: (0, 0)),
            pl.BlockSpec((1, D), lambda i, k: (0, 0)),
        ],
        out_specs=[pl.BlockSpec((tm, D), row), pl.BlockSpec((tm, D), row)],
        out_shape=[jax.ShapeDtypeStruct((T, D), F32), jax.ShapeDtypeStruct((T, D), BF16)],
        scratch_shapes=[pltpu.VMEM((tm + 2 * halo, tk), F32)],
        compiler_params=_cparams("parallel", "arbitrary"),
        name="ffn_down",
    )(up, up, up, up, conv_w, w_down, x, g_post.reshape(1, D), g_next.reshape(1, D))


def prepare_layer_params(p, l):
    w_in = p["w_in"][l]
    zeros = lambda r, c: jnp.zeros((r, c), w_in.dtype)
    a_end = 3 * A_WIDTH + SSM_WIDTH + SSM_WIDTH
    bc0 = a_end
    dt0 = bc0 + 2 * SSM_GROUPS * SSM_STATE
    c0 = dt0 + 2 * SSM_HEADS
    w_in_p = jnp.concatenate([
        w_in[:, :a_end], w_in[:, dt0:c0], zeros(D_MODEL, COL_CQ - COL_DT - 2 * SSM_HEADS),
        w_in[:, c0:], w_in[:, bc0:dt0]], axis=1).astype(BF16)
    up = p["ffn_w_up"][l]
    fpad = D_FF_PAD - D_FF
    w_up_p = jnp.concatenate([up[:, :D_FF], zeros(D_MODEL, fpad), up[:, D_FF:], zeros(D_MODEL, fpad)], axis=1).astype(BF16)
    w_down_p = jnp.concatenate([p["ffn_w_down"][l], zeros(fpad, D_MODEL)], axis=0).astype(BF16)
    lanes = lambda v: jnp.concatenate([v.reshape(-1), jnp.zeros((DT_LANES - 2 * SSM_HEADS,), F32)]).reshape(1, DT_LANES)
    conv_w = p["ssm_conv_w"][l]
    conv_b = p["ssm_conv_b"][l]
    return dict(
        w_in=w_in_p, w_up=w_up_p, w_down=w_down_p,
        w_out=p["w_out"][l].astype(BF16),
        mem_wq=p["mem_wq"][l].astype(BF16), mem_wo=p["mem_wo"][l].astype(BF16),
        mem_wkv=jnp.concatenate([p["mem_wk"][l], p["mem_wv"][l]], axis=1).astype(BF16),
        ffn_conv_w=jnp.concatenate([p["ffn_conv_w"][l], jnp.zeros((3, fpad), F32)], axis=1),
        conv_w_x=conv_w[:, :SSM_WIDTH], conv_b_x=conv_b[:SSM_WIDTH],
        conv_w_bc=conv_w[:, SSM_WIDTH:], conv_b_bc=conv_b[SSM_WIDTH:],
        dt_bias=lanes(p["ssm_dt_bias"][l]), a_log=lanes(p["ssm_a_log"][l]),
        d_row=jnp.repeat(p["ssm_d"][l], SSM_HEAD_DIM).reshape(1, SSM_WIDTH),
        ssm_norm=p["ssm_norm"][l].reshape(1, SSM_WIDTH),
        na_rpb=p["na_rpb"][l],
        mix_norm_pre=p["mix_norm_pre"][l], mix_norm_post=p["mix_norm_post"][l],
        mem_norm_pre=p["mem_norm_pre"][l], mem_norm_post=p["mem_norm_post"][l], mem_norm_kv=p["mem_norm_kv"][l],
        ffn_norm_pre=p["ffn_norm_pre"][l], ffn_norm_post=p["ffn_norm_post"][l],
    )


def hybrid_mixer(xn, lp, tab, B, L):
    proj = matmul(xn, lp["w_in"], 512, 1024, "in_proj").reshape(B, L, IN_COLS_PAD)
    outs, lses = [], []
    for g, (window, dil) in enumerate(DILATED_GROUPS):
        o, lse = dilated_attention(proj, tab, g, dil, window // (2 * dil))
        outs.append(o)
        lses.append(lse)
    o_a = combine_groups(outs, lses)
    xs = conv_silu(proj, COL_X, lp["conv_w_x"], lp["conv_b_x"])
    bc = conv_silu(proj, COL_B, lp["conv_w_bc"], lp["conv_b_bc"])
    o_b = ssd_mixer(proj, xs, bc, lp["dt_bias"], lp["a_log"], lp["d_row"], lp["ssm_norm"])
    o_c = neighbourhood_attention(proj, na_bias_table(lp["na_rpb"], L // GRID_W))
    return jnp.concatenate([o_a, o_b, o_c], axis=1)


def encoder_layer(x, xn, mem, lp, g_next, tab):
    B, L, D = x.shape
    T = B * L
    mix = hybrid_mixer(xn, lp, tab, B, L)
    x2, xn = matmul_residual(mix, lp["w_out"], x.reshape(T, D), lp["mix_norm_post"], lp["mem_norm_pre"], 256, 1024, "out_proj")
    mem_n = rmsnorm(mem.reshape(B * MEM_TOKENS, D), lp["mem_norm_kv"])
    kv = matmul(mem_n, lp["mem_wkv"], 256, 1024, "mem_kv_proj").reshape(B, MEM_TOKENS, 2 * MEM_WIDTH)
    x3, xn = memory_attention(xn.reshape(B, L, D), x2.reshape(B, L, D), kv, lp["mem_wq"], lp["mem_wo"],
                              lp["mem_norm_post"], lp["ffn_norm_pre"])
    up = matmul(xn.reshape(T, D), lp["w_up"], 512, 1024, "ffn_up")
    x4, xn = ffn_down(up, lp["ffn_conv_w"], lp["w_down"], x3.reshape(T, D), lp["ffn_norm_post"], g_next, L)
    return x4.reshape(B, L, D), xn


def kernel(x_prompt, x_sample, mem_prompt, mem_sample, mix_norm_pre, mix_norm_post, w_in, ssm_conv_w, ssm_conv_b,
           ssm_a_log, ssm_dt_bias, ssm_d, ssm_norm, na_rpb, w_out, mem_norm_pre, mem_norm_post, mem_norm_kv,
           mem_wq, mem_wk, mem_wv, mem_wo, ffn_norm_pre, ffn_norm_post, ffn_w_up, ffn_conv_w, ffn_w_down):
    p = dict(mix_norm_pre=mix_norm_pre, mix_norm_post=mix_norm_post, w_in=w_in, ssm_conv_w=ssm_conv_w,
             ssm_conv_b=ssm_conv_b, ssm_a_log=ssm_a_log, ssm_dt_bias=ssm_dt_bias, ssm_d=ssm_d, ssm_norm=ssm_norm,
             na_rpb=na_rpb, w_out=w_out, mem_norm_pre=mem_norm_pre, mem_norm_post=mem_norm_post,
             mem_norm_kv=mem_norm_kv, mem_wq=mem_wq, mem_wk=mem_wk, mem_wv=mem_wv, mem_wo=mem_wo,
             ffn_norm_pre=ffn_norm_pre, ffn_norm_post=ffn_norm_post, ffn_w_up=ffn_w_up, ffn_conv_w=ffn_conv_w,
             ffn_w_down=ffn_w_down)
    groups = [(x_prompt, mem_prompt), (x_sample, mem_sample)]
    tabs = [rotary_table(x.shape[1]) for x, _ in groups]
    xs = [x for x, _ in groups]
    xns = [rmsnorm(x.reshape(-1, D_MODEL), mix_norm_pre[0]) for x in xs]
    for l in range(DEPTH):
        lp = prepare_layer_params(p, l)
        g_next = mix_norm_pre[l + 1] if l + 1 < DEPTH else jnp.ones((D_MODEL,), F32)
        for gi, (_, mem) in enumerate(groups):
            xs[gi], xns[gi] = encoder_layer(xs[gi], xns[gi], mem, lp, g_next, tabs[gi])
    return (xs[0], xs[1])
```

```python
import functools
import math

import jax
import jax.numpy as jnp
import numpy as np
from jax import lax
from jax.experimental import pallas as pl
from jax.experimental.pallas import tpu as pltpu

F32 = jnp.float32
BF16 = jnp.bfloat16

D_MODEL = 4096
DEPTH = 4
HEAD_DIM = 128
A_WIDTH = 1536
SSM_WIDTH = 1536
NA_WIDTH = 1024
DILATED_GROUPS = ((128, 1), (512, 4), (2048, 16))
A_GROUP_HEADS = 4
A_OUT = A_GROUP_HEADS * HEAD_DIM
ROT_DIM = 32
ROPE_THETA = 500000.0
SSM_HEAD_DIM = 64
SSM_HEADS = 24
SSM_GROUPS = 8
SSM_GROUP_HEADS = SSM_HEADS // SSM_GROUPS
SSM_STATE = 128
SSM_CONV = 5
SSM_CHUNK = 128
NA_HEADS = 8
GRID_W = 64
NA_ROWS = 8
NA_COLS = 16
NA_KEY_ROWS = 10
MEM_TOKENS = 256
MEM_HEADS = 4
MEM_WIDTH = MEM_HEADS * HEAD_DIM
D_FF = 11008
D_FF_PAD = 11264
EPS = 1e-6

COL_A_GROUP = 3 * A_OUT
COL_Z, COL_X, COL_DT = 4608, 6144, 7680
COL_CQ, COL_CK, COL_CV, COL_B, COL_C = 8192, 9216, 10240, 11264, 12288
IN_COLS_PAD = 13312
DT_LANES = 128

VMEM_LIMIT = 56 * 1024 * 1024
NEG_BIG = -1e30


def _cparams(*sem):
    return pltpu.CompilerParams(dimension_semantics=sem, vmem_limit_bytes=VMEM_LIMIT)


def _rms(x, g):
    ms = jnp.mean(x * x, axis=-1, keepdims=True)
    return x * lax.rsqrt(ms + EPS) * g


def _rmsnorm_body(x_ref, g_ref, o_ref):
    o_ref[...] = _rms(x_ref[...], g_ref[...]).astype(o_ref.dtype)


def rmsnorm(x, g, tm=256):
    T, D = x.shape
    return pl.pallas_call(
        _rmsnorm_body,
        grid=(T // tm,),
        in_specs=[pl.BlockSpec((tm, D), lambda i: (i, 0)), pl.BlockSpec((1, D), lambda i: (0, 0))],
        out_specs=pl.BlockSpec((tm, D), lambda i: (i, 0)),
        out_shape=jax.ShapeDtypeStruct((T, D), BF16),
        compiler_params=_cparams("parallel"),
        name="rmsnorm",
    )(x, g.reshape(1, D))


def _mm_body(x_ref, w_ref, o_ref):
    o_ref[...] = jnp.dot(x_ref[...], w_ref[...], preferred_element_type=F32).astype(o_ref.dtype)


def matmul(x, w, tm, tn, name):
    T, K = x.shape
    N = w.shape[1]
    return pl.pallas_call(
        _mm_body,
        grid=(T // tm, N // tn),
        in_specs=[pl.BlockSpec((tm, K), lambda i, j: (i, 0)), pl.BlockSpec((K, tn), lambda i, j: (0, j))],
        out_specs=pl.BlockSpec((tm, tn), lambda i, j: (i, j)),
        out_shape=jax.ShapeDtypeStruct((T, N), BF16),
        compiler_params=_cparams("parallel", "arbitrary"),
        name=name,
    )(x, w)


ROW_SLAB = 128
COL_SLAB = 1024


def _row_slabs(ref):
    return [slice(r, r + ROW_SLAB) for r in range(0, ref.shape[0], ROW_SLAB)]


def _accumulate(xo_ref, lhs, w_ref):
    for c in range(0, xo_ref.shape[1], COL_SLAB):
        cols = slice(c, c + COL_SLAB)
        xo_ref[:, cols] += jnp.dot(lhs, w_ref[:, cols], preferred_element_type=F32)


def _residual_epilogue(xo_ref, x_ref, gp_ref, gn_ref, xn_ref):
    for rows in _row_slabs(xo_ref):
        x_new = x_ref[rows, :] + _rms(xo_ref[rows, :], gp_ref[...])
        xo_ref[rows, :] = x_new
        xn_ref[rows, :] = _rms(x_new, gn_ref[...]).astype(xn_ref.dtype)


def _out_proj_body(oa_ref, ob_ref, oc_ref, w_ref, x_ref, gp_ref, gn_ref, xo_ref, xn_ref, *, ka, kb):
    k = pl.program_id(1)

    @pl.when(k == 0)
    def _():
        xo_ref[...] = jnp.zeros_like(xo_ref)

    @pl.when(k < ka)
    def _():
        _accumulate(xo_ref, oa_ref[...], w_ref)

    @pl.when((k >= ka) & (k < ka + kb))
    def _():
        _accumulate(xo_ref, ob_ref[...], w_ref)

    @pl.when(k >= ka + kb)
    def _():
        _accumulate(xo_ref, oc_ref[...], w_ref)

    @pl.when(k == pl.num_programs(1) - 1)
    def _():
        _residual_epilogue(xo_ref, x_ref, gp_ref, gn_ref, xn_ref)


def mixer_out_proj(o_a, o_b, o_c, w, x, g_post, g_next, tm=512, tk=512):
    T = x.shape[0]
    D = w.shape[1]
    ka, kb, kc = o_a.shape[1] // tk, o_b.shape[1] // tk, o_c.shape[1] // tk
    row = lambda i, k: (i, 0)
    return pl.pallas_call(
        functools.partial(_out_proj_body, ka=ka, kb=kb),
        grid=(T // tm, ka + kb + kc),
        in_specs=[
            pl.BlockSpec((tm, tk), lambda i, k: (i, jnp.clip(k, 0, ka - 1))),
            pl.BlockSpec((tm, tk), lambda i, k: (i, jnp.clip(k - ka, 0, kb - 1))),
            pl.BlockSpec((tm, tk), lambda i, k: (i, jnp.clip(k - ka - kb, 0, kc - 1))),
            pl.BlockSpec((tk, D), lambda i, k: (k, 0)),
            pl.BlockSpec((tm, D), row),
            pl.BlockSpec((1, D), lambda i, k: (0, 0)),
            pl.BlockSpec((1, D), lambda i, k: (0, 0)),
        ],
        out_specs=[pl.BlockSpec((tm, D), row), pl.BlockSpec((tm, D), row)],
        out_shape=[jax.ShapeDtypeStruct((T, D), F32), jax.ShapeDtypeStruct((T, D), BF16)],
        compiler_params=_cparams("parallel", "arbitrary"),
        name="out_proj",
    )(o_a, o_b, o_c, w, x, g_post.reshape(1, D), g_next.reshape(1, D))


def rotary_table(L):
    half = ROT_DIM // 2
    inv_freq = jnp.exp(-math.log(ROPE_THETA) * jnp.arange(half, dtype=F32) / half)
    ang = jnp.arange(L, dtype=F32)[:, None] * inv_freq[None, :]
    cos, sin = jnp.cos(ang), jnp.sin(ang)
    ones = jnp.ones((L, HEAD_DIM - ROT_DIM), F32)
    zeros = jnp.zeros((L, HEAD_DIM - ROT_DIM), F32)
    return jnp.concatenate([cos, cos, ones, -sin, sin, zeros], axis=1)


def _rope(t, tab):
    lane = lax.broadcasted_iota(jnp.int32, t.shape, 1)
    partner = jnp.where(lane < ROT_DIM // 2, pltpu.roll(t, HEAD_DIM - ROT_DIM // 2, 1), pltpu.roll(t, ROT_DIM // 2, 1))
    return t * tab[:, :HEAD_DIM] + partner * tab[:, HEAD_DIM:]


def _dilated_body(q_ref, kp_ref, kc_ref, kn_ref, vp_ref, vc_ref, vn_ref, tp_ref, tc_ref, tn_ref,
                  o_ref, lse_ref, *, radius, m_len):
    m = pl.program_id(2)
    tq = q_ref.shape[0]
    row = lax.broadcasted_iota(jnp.int32, (tq, 3 * tq), 0)
    col = lax.broadcasted_iota(jnp.int32, (tq, 3 * tq), 1)
    kpos = (m - 1) * tq + col
    rel = col - tq - row
    valid = (jnp.abs(rel) <= radius) & (kpos >= 0) & (kpos < m_len)
    scale = HEAD_DIM ** -0.5
    nt = (((1,), (1,)), ((), ()))
    for h in range(A_GROUP_HEADS):
        hs = slice(h * HEAD_DIM, (h + 1) * HEAD_DIM)
        q = _rope(q_ref[:, hs].astype(F32), tc_ref[...]).astype(BF16)
        parts = []
        for k_ref, t_ref in ((kp_ref, tp_ref), (kc_ref, tc_ref), (kn_ref, tn_ref)):
            kk = _rope(k_ref[:, hs].astype(F32), t_ref[...]).astype(BF16)
            parts.append(lax.dot_general(q, kk, nt, preferred_element_type=F32))
        s = jnp.concatenate(parts, axis=1) * scale
        s = jnp.where(valid, s, NEG_BIG)
        mx = jnp.max(s, axis=-1, keepdims=True)
        p = jnp.exp(s - mx)
        l = jnp.sum(p, axis=-1, keepdims=True)
        pb = p.astype(BF16)
        o = jnp.dot(pb[:, :tq], vp_ref[:, hs], preferred_element_type=F32)
        o += jnp.dot(pb[:, tq:2 * tq], vc_ref[:, hs], preferred_element_type=F32)
        o += jnp.dot(pb[:, 2 * tq:], vn_ref[:, hs], preferred_element_type=F32)
        o_ref[:, hs] = o / l
        lse_ref[:, hs] = jnp.broadcast_to(mx + jnp.log(l), (tq, HEAD_DIM))


def dilated_attention(proj, tab, g, dil, radius):
    B, L, C = proj.shape
    M = L // dil
    tq = 128
    nmb = M // tq
    if dil == 1:
        pv, ncb, cq = proj, C // A_OUT, g * 3
    else:
        pv = proj[:, :, g * COL_A_GROUP:(g + 1) * COL_A_GROUP].reshape(B, M, dil * COL_A_GROUP)
        ncb, cq = 3, 0
    ck, cv = cq + 1, cq + 2
    tv = tab.reshape(M, dil * 2 * HEAD_DIM)

    def blk(off, col0):
        return pl.BlockSpec((None, tq, A_OUT),
                            lambda b, r, m: (b, jnp.clip(m + off, 0, nmb - 1), r * ncb + col0))

    def tblk(off):
        return pl.BlockSpec((tq, 2 * HEAD_DIM), lambda b, r, m: (jnp.clip(m + off, 0, nmb - 1), r))

    out_spec = pl.BlockSpec((None, tq, A_OUT), lambda b, r, m: (b, m, r))
    o, lse = pl.pallas_call(
        functools.partial(_dilated_body, radius=radius, m_len=M),
        grid=(B, dil, nmb),
        in_specs=[blk(0, cq), blk(-1, ck), blk(0, ck), blk(1, ck), blk(-1, cv), blk(0, cv), blk(1, cv),
                  tblk(-1), tblk(0), tblk(1)],
        out_specs=[out_spec, out_spec],
        out_shape=[jax.ShapeDtypeStruct((B, M, dil * A_OUT), F32)] * 2,
        compiler_params=_cparams("parallel", "parallel", "arbitrary"),
        name=f"dilated_attn_d{dil}",
    )(pv, pv, pv, pv, pv, pv, pv, tv, tv, tv)
    return o.reshape(B * L, A_OUT), lse.reshape(B * L, A_OUT)


def _combine_body(o0, o1, o2, l0, l1, l2, out_ref):
    a, b, c = l0[...], l1[...], l2[...]
    mx = jnp.maximum(jnp.maximum(a, b), c)
    ea, eb, ec = jnp.exp(a - mx), jnp.exp(b - mx), jnp.exp(c - mx)
    den = ea + eb + ec
    out_ref[...] = ((ea * o0[...] + eb * o1[...] + ec * o2[...]) / den).astype(out_ref.dtype)


def combine_groups(outs, lses, tm=512):
    T, W = outs[0].shape
    spec = pl.BlockSpec((tm, W), lambda i: (i, 0))
    return pl.pallas_call(
        _combine_body,
        grid=(T // tm,),
        in_specs=[spec] * 6,
        out_specs=spec,
        out_shape=jax.ShapeDtypeStruct((T, W), BF16),
        compiler_params=_cparams("parallel"),
        name="dilated_combine",
    )(*outs, *lses)


def _conv_silu_body(x_ref, xp_ref, xn_ref, w_ref, b_ref, o_ref, ext_ref, *, halo):
    i = pl.program_id(1)
    tl = x_ref.shape[0]
    first = i == 0
    last = i == pl.num_programs(1) - 1
    ext_ref[0:halo, :] = jnp.where(first, 0.0, xp_ref[...].astype(F32))
    ext_ref[halo:halo + tl, :] = x_ref[...].astype(F32)
    ext_ref[halo + tl:, :] = jnp.where(last, 0.0, xn_ref[...].astype(F32))
    acc = jnp.broadcast_to(b_ref[...], (tl, b_ref.shape[1]))
    pad = SSM_CONV // 2
    for k in range(SSM_CONV):
        acc = acc + w_ref[k:k + 1, :] * ext_ref[halo - pad + k:halo - pad + k + tl, :]
    o_ref[...] = (acc / (1.0 + jnp.exp(-acc))).astype(o_ref.dtype)


def conv_silu(proj, col0, w, b, tl=512, tc=512):
    B, L, _ = proj.shape
    C = w.shape[1]
    halo = 16
    cb0 = col0 // tc
    hb = tl // halo
    nhb = L // halo
    return pl.pallas_call(
        functools.partial(_conv_silu_body, halo=halo),
        grid=(B, L // tl, C // tc),
        in_specs=[
            pl.BlockSpec((None, tl, tc), lambda b, i, j: (b, i, cb0 + j)),
            pl.BlockSpec((None, halo, tc), lambda b, i, j: (b, jnp.maximum(i * hb - 1, 0), cb0 + j)),
            pl.BlockSpec((None, halo, tc), lambda b, i, j: (b, jnp.minimum((i + 1) * hb, nhb - 1), cb0 + j)),
            pl.BlockSpec((SSM_CONV, tc), lambda b, i, j: (0, j)),
            pl.BlockSpec((1, tc), lambda b, i, j: (0, j)),
        ],
        out_specs=pl.BlockSpec((None, tl, tc), lambda b, i, j: (b, i, j)),
        out_shape=jax.ShapeDtypeStruct((B, L, C), BF16),
        scratch_shapes=[pltpu.VMEM((tl + 2 * halo, tc), F32)],
        compiler_params=_cparams("parallel", "parallel", "arbitrary"),
        name="ssm_conv_silu",
    )(proj, proj, proj, w, b.reshape(1, C))


def _split3(a):
    hi = a.astype(BF16)
    r1 = a - hi.astype(F32)
    mid = r1.astype(BF16)
    lo = (r1 - mid.astype(F32)).astype(BF16)
    return jnp.concatenate([hi, mid, lo], axis=1)


def _ssd_chunk_terms(dt_ref, bias_ref, alog_ref):
    Q = SSM_CHUNK
    lane = lax.broadcasted_iota(jnp.int32, (Q, DT_LANES), 1)
    row = lax.broadcasted_iota(jnp.int32, (Q, DT_LANES), 0)
    v = dt_ref[...].astype(F32) + bias_ref[...]
    dt = jnp.maximum(v, 0.0) + jnp.log1p(jnp.exp(-jnp.abs(v)))
    a_row = -jnp.exp(alog_ref[...])
    da = jnp.where(lane < 2 * SSM_HEADS, dt * a_row, 0.0)
    pre, suf = da, da
    s = 1
    while s < Q:
        pre = pre + jnp.where(row >= s, pltpu.roll(pre, s, 0), 0.0)
        suf = suf + jnp.where(row < Q - s, pltpu.roll(suf, Q - s, 0), 0.0)
        s *= 2
    is_fwd = lane < SSM_HEADS
    cs = jnp.where(is_fwd, pre, suf)
    tot = jnp.where(is_fwd[:1], cs[Q - 1:Q, :], cs[0:1, :])
    e = jnp.exp(cs)
    w = jnp.exp(tot - cs) * dt
    return dt, cs, e, w


def _ssd_fwd_body(x_ref, bc_ref, dt_ref, bias_ref, alog_ref, d_ref, sel_ref, exp_ref, y_ref, h_ref):
    Q, G, R, P, N = SSM_CHUNK, SSM_GROUPS, SSM_GROUP_HEADS, SSM_HEAD_DIM, SSM_STATE
    GP = R * P

    @pl.when(pl.program_id(1) == 0)
    def _():
        h_ref[...] = jnp.zeros_like(h_ref)

    dt, cs, e, w = _ssd_chunk_terms(dt_ref, bias_ref, alog_ref)
    cs_t = cs.T
    dt_t = dt.T
    colb = jnp.dot(_split3(cs), sel_ref[...], preferred_element_type=F32)
    e_x = jnp.dot(_split3(e), exp_ref[:, :SSM_WIDTH], preferred_element_type=F32)
    w_x = jnp.dot(_split3(w), exp_ref[:, :SSM_WIDTH], preferred_element_type=F32)
    x = x_ref[...]
    xf = x.astype(F32)
    xw = (xf * w_x).astype(BF16)
    tri = lax.broadcasted_iota(jnp.int32, (Q, Q), 0) >= lax.broadcasted_iota(jnp.int32, (Q, Q), 1)
    nt = (((1,), (1,)), ((), ()))
    ys, y_offs = [], []
    for g in range(G):
        bg = bc_ref[:, g * N:(g + 1) * N]
        cg = bc_ref[:, G * N + g * N:G * N + (g + 1) * N]
        cb = lax.dot_general(cg, bg, nt, preferred_element_type=F32)
        for r in range(R):
            h = g * R + r
            hb = SSM_HEADS + h
            arg = jnp.where(tri, colb[:, h * Q:(h + 1) * Q] - cs_t[h:h + 1, :],
                            colb[:, hb * Q:(hb + 1) * Q] - cs_t[hb:hb + 1, :])
            dtrow = jnp.where(tri, dt_t[h:h + 1, :], dt_t[hb:hb + 1, :])
            wm = (cb * jnp.exp(arg) * dtrow).astype(BF16)
            ys.append(jnp.dot(wm, x[:, h * P:(h + 1) * P], preferred_element_type=F32))
        gs = slice(g * GP, (g + 1) * GP)
        hg = h_ref[g]
        y_offs.append(jnp.dot(cg, hg.astype(BF16), preferred_element_type=F32))
        bg_t = bg.astype(F32).T.astype(BF16)
        h_ref[g] = hg * e_x[Q - 1:Q, gs] + jnp.dot(bg_t, xw[:, gs], preferred_element_type=F32)
    y_ref[...] = jnp.concatenate(ys, axis=1) + jnp.concatenate(y_offs, axis=1) * e_x + d_ref[...] * xf


def _ssd_bwd_body(x_ref, bc_ref, dt_ref, bias_ref, alog_ref, exp_ref, yf_ref, z_ref, gain_ref, o_ref, h_ref):
    Q, G, R, P, N = SSM_CHUNK, SSM_GROUPS, SSM_GROUP_HEADS, SSM_HEAD_DIM, SSM_STATE
    GP = R * P

    @pl.when(pl.program_id(1) == 0)
    def _():
        h_ref[...] = jnp.zeros_like(h_ref)

    _, _, e, w = _ssd_chunk_terms(dt_ref, bias_ref, alog_ref)
    e_x = jnp.dot(_split3(e), exp_ref[:, SSM_WIDTH:], preferred_element_type=F32)
    w_x = jnp.dot(_split3(w), exp_ref[:, SSM_WIDTH:], preferred_element_type=F32)
    xw = (x_ref[...].astype(F32) * w_x).astype(BF16)
    ys = []
    for g in range(G):
        bg = bc_ref[:, g * N:(g + 1) * N]
        cg = bc_ref[:, G * N + g * N:G * N + (g + 1) * N]
        gs = slice(g * GP, (g + 1) * GP)
        hg = h_ref[g]
        ys.append(jnp.dot(cg, hg.astype(BF16), preferred_element_type=F32))
        bg_t = bg.astype(F32).T.astype(BF16)
        h_ref[g] = hg * e_x[0:1, gs] + jnp.dot(bg_t, xw[:, gs], preferred_element_type=F32)
    y = yf_ref[...] + jnp.concatenate(ys, axis=1) * e_x
    z = z_ref[...].astype(F32)
    y = y * (z / (1.0 + jnp.exp(-z)))
    o_ref[...] = _rms(y, gain_ref[...]).astype(o_ref.dtype)


def _ssd_constants():
    sel = np.zeros((DT_LANES, 2 * SSM_HEADS * SSM_CHUNK), np.float32)
    for h in range(2 * SSM_HEADS):
        sel[h, h * SSM_CHUNK:(h + 1) * SSM_CHUNK] = 1.0
    ex = np.zeros((DT_LANES, 2 * SSM_WIDTH), np.float32)
    for h in range(2 * SSM_HEADS):
        ex[h, h * SSM_HEAD_DIM:(h + 1) * SSM_HEAD_DIM] = 1.0
    return jnp.asarray(np.tile(sel, (3, 1)), BF16), jnp.asarray(np.tile(ex, (3, 1)), BF16)


def ssd_mixer(proj, xs, bc, dt_bias, a_log, d_row, gain):
    B, L, _ = proj.shape
    Q = SSM_CHUNK
    nc = L // Q
    sel, ex = _ssd_constants()
    full = lambda shape: pl.BlockSpec(shape, lambda b, c: (0,) * len(shape))
    state = pltpu.VMEM((SSM_GROUPS, SSM_STATE, SSM_GROUP_HEADS * SSM_HEAD_DIM), F32)

    def chunk(width, col_block, rev):
        if rev:
            return pl.BlockSpec((None, Q, width), lambda b, c: (b, nc - 1 - c, col_block))
        return pl.BlockSpec((None, Q, width), lambda b, c: (b, c, col_block))

    y_f = pl.pallas_call(
        _ssd_fwd_body,
        grid=(B, nc),
        in_specs=[chunk(SSM_WIDTH, 0, False), chunk(2 * SSM_GROUPS * SSM_STATE, 0, False),
                  chunk(DT_LANES, COL_DT // DT_LANES, False),
                  full((1, DT_LANES)), full((1, DT_LANES)), full((1, SSM_WIDTH)),
                  full(sel.shape), full(ex.shape)],
        out_specs=chunk(SSM_WIDTH, 0, False),
        out_shape=jax.ShapeDtypeStruct((B, L, SSM_WIDTH), F32),
        scratch_shapes=[state],
        compiler_params=_cparams("parallel", "arbitrary"),
        name="ssd_fwd",
    )(xs, bc, proj, dt_bias, a_log, d_row, sel, ex)
    o_b = pl.pallas_call(
        _ssd_bwd_body,
        grid=(B, nc),
        in_specs=[chunk(SSM_WIDTH, 0, True), chunk(2 * SSM_GROUPS * SSM_STATE, 0, True),
                  chunk(DT_LANES, COL_DT // DT_LANES, True),
                  full((1, DT_LANES)), full((1, DT_LANES)), full(ex.shape),
                  chunk(SSM_WIDTH, 0, True), chunk(SSM_WIDTH, COL_Z // SSM_WIDTH, True), full((1, SSM_WIDTH))],
        out_specs=chunk(SSM_WIDTH, 0, True),
        out_shape=jax.ShapeDtypeStruct((B, L, SSM_WIDTH), BF16),
        scratch_shapes=[state],
        compiler_params=_cparams("parallel", "arbitrary"),
        name="ssd_bwd",
    )(xs, bc, proj, dt_bias, a_log, ex, y_f, proj, gain)
    return o_b.reshape(B * L, SSM_WIDTH)


def na_bias_table(rpb, rows):
    nrb = rows // 2
    ii = np.array([0, 1, 2, nrb - 2, nrb - 1])
    qrow = ii[:, None] * 2 + np.arange(2)[None, :]
    kstart = np.clip(ii * 2 - NA_ROWS // 2, 0, rows - NA_KEY_ROWS)
    krow = kstart[:, None] + np.arange(NA_KEY_ROWS)[None, :]
    wr = np.clip(qrow - NA_ROWS // 2, 0, rows - NA_ROWS)
    row_ok = (krow[:, None, :] >= wr[:, :, None]) & (krow[:, None, :] < wr[:, :, None] + NA_ROWS)
    row_off = np.clip(krow[:, None, :] - qrow[:, :, None] + NA_ROWS - 1, 0, 2 * NA_ROWS - 2)
    qcol = np.arange(GRID_W)
    wc = np.clip(qcol - NA_COLS // 2, 0, GRID_W - NA_COLS)
    col_ok = (qcol[None, :] >= wc[:, None]) & (qcol[None, :] < wc[:, None] + NA_COLS)
    col_off = np.clip(qcol[None, :] - qcol[:, None] + NA_COLS - 1, 0, 2 * NA_COLS - 2)
    row_sel = (row_off[..., None] == np.arange(2 * NA_ROWS - 1)).astype(np.float32)
    col_sel = (col_off[..., None] == np.arange(2 * NA_COLS - 1)).astype(np.float32)
    hi = lax.Precision.HIGHEST
    t1 = jnp.einsum('pqkr,hrc->phqkc', row_sel, rpb.astype(F32), precision=hi)
    bias = jnp.einsum('phqkc,xyc->phqxky', t1, col_sel, precision=hi)
    mask = row_ok[:, None, :, None, :, None] & col_ok[None, None, None, :, None, :]
    tab = jnp.where(mask, bias, NEG_BIG)
    return tab.reshape(5, rpb.shape[0], 2 * GRID_W, NA_KEY_ROWS * GRID_W)


def _na_body(q_ref, k0, k1, k2, k3, k4, v0, v1, v2, v3, v4, bias_ref, o_ref):
    scale = HEAD_DIM ** -0.5
    nt = (((1,), (1,)), ((), ()))
    k_refs = (k0, k1, k2, k3, k4)
    v_refs = (v0, v1, v2, v3, v4)
    tq = q_ref.shape[0]
    for h in range(NA_HEADS):
        hs = slice(h * HEAD_DIM, (h + 1) * HEAD_DIM)
        q = q_ref[:, hs]
        s = jnp.concatenate([lax.dot_general(q, kr[:, hs], nt, preferred_element_type=F32) for kr in k_refs], axis=1)
        s = s * scale + bias_ref[h]
        mx = jnp.max(s, axis=-1, keepdims=True)
        p = jnp.exp(s - mx)
        l = jnp.sum(p, axis=-1, keepdims=True)
        pb = p.astype(BF16)
        o = jnp.dot(pb[:, :tq], v_refs[0][:, hs], preferred_element_type=F32)
        for j in range(1, 5):
            o += jnp.dot(pb[:, j * tq:(j + 1) * tq], v_refs[j][:, hs], preferred_element_type=F32)
        o_ref[:, hs] = (o / l).astype(o_ref.dtype)


def neighbourhood_attention(proj, bias_tab):
    B, L, _ = proj.shape
    tq = 2 * GRID_W
    nrb = L // tq
    W = NA_WIDTH

    def pattern(i):
        return jnp.where(i < 2, i, jnp.where(i >= nrb - 2, i - (nrb - 5), 2))

    def kv(j, col):
        return pl.BlockSpec((None, tq, W), lambda b, i: (b, jnp.clip(i - 2, 0, nrb - 5) + j, col))

    cq, ck, cv = COL_CQ // W, COL_CK // W, COL_CV // W
    o = pl.pallas_call(
        _na_body,
        grid=(B, nrb),
        in_specs=[pl.BlockSpec((None, tq, W), lambda b, i: (b, i, cq))]
        + [kv(j, ck) for j in range(5)] + [kv(j, cv) for j in range(5)]
        + [pl.BlockSpec((None, NA_HEADS, tq, 5 * tq), lambda b, i: (pattern(i), 0, 0, 0))],
        out_specs=pl.BlockSpec((None, tq, W), lambda b, i: (b, i, 0)),
        out_shape=jax.ShapeDtypeStruct((B, L, W), BF16),
        compiler_params=_cparams("parallel", "arbitrary"),
        name="neighbourhood_attn",
    )(*([proj] * 11), bias_tab)
    return o.reshape(B * L, W)


def _mem_attn_body(xn_ref, x_ref, kv_ref, wq_ref, wo_ref, gp_ref, gn_ref, xo_ref, xnn_ref):
    scale = HEAD_DIM ** -0.5
    nt = (((1,), (1,)), ((), ()))
    q = jnp.dot(xn_ref[...], wq_ref[...], preferred_element_type=F32).astype(BF16)
    outs = []
    for h in range(MEM_HEADS):
        hs = slice(h * HEAD_DIM, (h + 1) * HEAD_DIM)
        vs = slice(MEM_WIDTH + h * HEAD_DIM, MEM_WIDTH + (h + 1) * HEAD_DIM)
        s = lax.dot_general(q[:, hs], kv_ref[:, hs], nt, preferred_element_type=F32) * scale
        mx = jnp.max(s, axis=-1, keepdims=True)
        p = jnp.exp(s - mx)
        l = jnp.sum(p, axis=-1, keepdims=True)
        o = jnp.dot(p.astype(BF16), kv_ref[:, vs], preferred_element_type=F32) / l
        outs.append(o.astype(BF16))
    xo_ref[...] = jnp.dot(jnp.concatenate(outs, axis=1), wo_ref[...], preferred_element_type=F32)
    _residual_epilogue(xo_ref, x_ref, gp_ref, gn_ref, xnn_ref)


def memory_attention(xn, x, kv, wq, wo, g_post, g_next, tm=256):
    B, L, D = x.shape
    tok = pl.BlockSpec((None, tm, D), lambda b, i: (b, i, 0))
    full = lambda shape: pl.BlockSpec(shape, lambda b, i: (0,) * len(shape))
    return pl.pallas_call(
        _mem_attn_body,
        grid=(B, L // tm),
        in_specs=[tok, tok, pl.BlockSpec((None, MEM_TOKENS, 2 * MEM_WIDTH), lambda b, i: (b, 0, 0)),
                  full((D, MEM_WIDTH)), full((MEM_WIDTH, D)), full((1, D)), full((1, D))],
        out_specs=[tok, tok],
        out_shape=[jax.ShapeDtypeStruct((B, L, D), F32), jax.ShapeDtypeStruct((B, L, D), BF16)],
        compiler_params=_cparams("parallel", "arbitrary"),
        name="memory_attn",
    )(xn, x, kv, wq, wo, g_post.reshape(1, D), g_next.reshape(1, D))


def _ffn_up_body(xn_ref, xp_ref, xnx_ref, wg_ref, wv_ref, cw_ref, o_ref, xe_ref, ext_ref, *, halo, tiles_per_seq):
    i = pl.program_id(0)
    tm = xn_ref.shape[0]

    @pl.when(pl.program_id(1) == 0)
    def _():
        xe_ref[0:halo, :] = xp_ref[...]
        xe_ref[halo:halo + tm, :] = xn_ref[...]
        xe_ref[halo + tm:, :] = xnx_ref[...]

    pos = lax.rem(i, tiles_per_seq)
    gate = jnp.dot(xe_ref[...], wg_ref[...], preferred_element_type=F32)
    ext_ref[0:halo, :] = jnp.where(pos == 0, 0.0, gate[0:halo])
    ext_ref[halo:halo + tm, :] = gate[halo:halo + tm]
    ext_ref[halo + tm:, :] = jnp.where(pos == tiles_per_seq - 1, 0.0, gate[halo + tm:])
    val = jnp.dot(xn_ref[...], wv_ref[...], preferred_element_type=F32)
    conv = (cw_ref[0:1, :] * ext_ref[halo - 1:halo - 1 + tm, :]
            + cw_ref[1:2, :] * ext_ref[halo:halo + tm, :]
            + cw_ref[2:3, :] * ext_ref[halo + 1:halo + 1 + tm, :])
    cdf = 0.5 * (1.0 + jnp.tanh(math.sqrt(2.0 / math.pi) * (conv + 0.044715 * (conv * conv * conv))))
    o_ref[...] = (conv * cdf * val).astype(o_ref.dtype)


def ffn_up_gated(xn, w_up, conv_w, seq_len, tm=512, tn=512):
    T, D = xn.shape
    F = w_up.shape[1] // 2
    halo = 16
    nj = F // tn
    hb = tm // halo
    nhb = T // halo
    return pl.pallas_call(
        functools.partial(_ffn_up_body, halo=halo, tiles_per_seq=seq_len // tm),
        grid=(T // tm, nj),
        in_specs=[
            pl.BlockSpec((tm, D), lambda i, j: (i, 0)),
            pl.BlockSpec((halo, D), lambda i, j: (jnp.maximum(i * hb - 1, 0), 0)),
            pl.BlockSpec((halo, D), lambda i, j: (jnp.minimum((i + 1) * hb, nhb - 1), 0)),
            pl.BlockSpec((D, tn), lambda i, j: (0, j)),
            pl.BlockSpec((D, tn), lambda i, j: (0, nj + j)),
            pl.BlockSpec((3, tn), lambda i, j: (0, j)),
        ],
        out_specs=pl.BlockSpec((tm, tn), lambda i, j: (i, j)),
        out_shape=jax.ShapeDtypeStruct((T, F), BF16),
        scratch_shapes=[pltpu.VMEM((tm + 2 * halo, D), BF16), pltpu.VMEM((tm + 2 * halo, tn), F32)],
        compiler_params=_cparams("parallel", "arbitrary"),
        name="ffn_up",
    )(xn, xn, xn, w_up, w_up, conv_w)


def _mm_res_body(h_ref, w_ref, x_ref, gp_ref, gn_ref, xo_ref, xn_ref):
    @pl.when(pl.program_id(1) == 0)
    def _():
        xo_ref[...] = jnp.zeros_like(xo_ref)

    _accumulate(xo_ref, h_ref[...], w_ref)

    @pl.when(pl.program_id(1) == pl.num_programs(1) - 1)
    def _():
        _residual_epilogue(xo_ref, x_ref, gp_ref, gn_ref, xn_ref)


def matmul_residual(h, w, x, g_post, g_next, name, tm=512, tk=512):
    T, K = h.shape
    D = w.shape[1]
    row = lambda i, k: (i, 0)
    return pl.pallas_call(
        _mm_res_body,
        grid=(T // tm, K // tk),
        in_specs=[
            pl.BlockSpec((tm, tk), lambda i, k: (i, k)),
            pl.BlockSpec((tk, D), lambda i, k: (k, 0)),
            pl.BlockSpec((tm, D), row),
            pl.BlockSpec((1, D), lambda i, k: (0, 0)),
            pl.BlockSpec((1, D), lambda i, k: (0, 0)),
        ],
        out_specs=[pl.BlockSpec((tm, D), row), pl.BlockSpec((tm, D), row)],
        out_shape=[jax.ShapeDtypeStruct((T, D), F32), jax.ShapeDtypeStruct((T, D), BF16)],
        compiler_params=_cparams("parallel", "arbitrary"),
        name=name,
    )(h, w, x, g_post.reshape(1, D), g_next.reshape(1, D))


def prepare_layer_params(p, l):
    w_in = p["w_in"][l]
    zeros = lambda r, c: jnp.zeros((r, c), w_in.dtype)
    a_end = 3 * A_WIDTH + SSM_WIDTH + SSM_WIDTH
    bc0 = a_end
    dt0 = bc0 + 2 * SSM_GROUPS * SSM_STATE
    c0 = dt0 + 2 * SSM_HEADS
    a_cols = [w_in[:, s * A_WIDTH + g * A_OUT:s * A_WIDTH + (g + 1) * A_OUT]
              for g in range(len(DILATED_GROUPS)) for s in range(3)]
    w_in_p = jnp.concatenate(
        a_cols + [w_in[:, 3 * A_WIDTH:a_end], w_in[:, dt0:c0], zeros(D_MODEL, COL_CQ - COL_DT - 2 * SSM_HEADS),
                  w_in[:, c0:], w_in[:, bc0:dt0]], axis=1).astype(BF16)
    up = p["ffn_w_up"][l]
    fpad = D_FF_PAD - D_FF
    w_up_p = jnp.concatenate([up[:, :D_FF], zeros(D_MODEL, fpad), up[:, D_FF:], zeros(D_MODEL, fpad)], axis=1).astype(BF16)
    w_down_p = jnp.concatenate([p["ffn_w_down"][l], zeros(fpad, D_MODEL)], axis=0).astype(BF16)
    lanes = lambda v: jnp.concatenate([v.reshape(-1), jnp.zeros((DT_LANES - 2 * SSM_HEADS,), F32)]).reshape(1, DT_LANES)
    conv_w = p["ssm_conv_w"][l]
    conv_b = p["ssm_conv_b"][l]
    return dict(
        w_in=w_in_p, w_up=w_up_p, w_down=w_down_p,
        w_out=p["w_out"][l].astype(BF16),
        mem_wq=p["mem_wq"][l].astype(BF16), mem_wo=p["mem_wo"][l].astype(BF16),
        mem_wkv=jnp.concatenate([p["mem_wk"][l], p["mem_wv"][l]], axis=1).astype(BF16),
        ffn_conv_w=jnp.concatenate([p["ffn_conv_w"][l], jnp.zeros((3, fpad), F32)], axis=1),
        conv_w_x=conv_w[:, :SSM_WIDTH], conv_b_x=conv_b[:SSM_WIDTH],
        conv_w_bc=conv_w[:, SSM_WIDTH:], conv_b_bc=conv_b[SSM_WIDTH:],
        dt_bias=lanes(p["ssm_dt_bias"][l]), a_log=lanes(p["ssm_a_log"][l]),
        d_row=jnp.repeat(p["ssm_d"][l], SSM_HEAD_DIM).reshape(1, SSM_WIDTH),
        ssm_norm=p["ssm_norm"][l].reshape(1, SSM_WIDTH),
        na_rpb=p["na_rpb"][l],
        mix_norm_pre=p["mix_norm_pre"][l], mix_norm_post=p["mix_norm_post"][l],
        mem_norm_pre=p["mem_norm_pre"][l], mem_norm_post=p["mem_norm_post"][l], mem_norm_kv=p["mem_norm_kv"][l],
        ffn_norm_pre=p["ffn_norm_pre"][l], ffn_norm_post=p["ffn_norm_post"][l],
    )


def hybrid_mixer(xn, lp, tab, B, L):
    proj = matmul(xn, lp["w_in"], 512, 1024, "in_proj").reshape(B, L, IN_COLS_PAD)
    outs, lses = [], []
    for g, (window, dil) in enumerate(DILATED_GROUPS):
        o, lse = dilated_attention(proj, tab, g, dil, window // (2 * dil))
        outs.append(o)
        lses.append(lse)
    o_a = combine_groups(outs, lses)
    xs = conv_silu(proj, COL_X, lp["conv_w_x"], lp["conv_b_x"])
    bc = conv_silu(proj, COL_B, lp["conv_w_bc"], lp["conv_b_bc"])
    o_b = ssd_mixer(proj, xs, bc, lp["dt_bias"], lp["a_log"], lp["d_row"], lp["ssm_norm"])
    o_c = neighbourhood_attention(proj, na_bias_table(lp["na_rpb"], L // GRID_W))
    return o_a, o_b, o_c


def encoder_layer(x, xn, mem, lp, g_next, tab):
    B, L, D = x.shape
    T = B * L
    o_a, o_b, o_c = hybrid_mixer(xn, lp, tab, B, L)
    x2, xn = mixer_out_proj(o_a, o_b, o_c, lp["w_out"], x.reshape(T, D), lp["mix_norm_post"], lp["mem_norm_pre"])
    mem_n = rmsnorm(mem.reshape(B * MEM_TOKENS, D), lp["mem_norm_kv"])
    kv = matmul(mem_n, lp["mem_wkv"], 256, 1024, "mem_kv_proj").reshape(B, MEM_TOKENS, 2 * MEM_WIDTH)
    x3, xn = memory_attention(xn.reshape(B, L, D), x2.reshape(B, L, D), kv, lp["mem_wq"], lp["mem_wo"],
                              lp["mem_norm_post"], lp["ffn_norm_pre"])
    act = ffn_up_gated(xn.reshape(T, D), lp["w_up"], lp["ffn_conv_w"], L)
    x4, xn = matmul_residual(act, lp["w_down"], x3.reshape(T, D), lp["ffn_norm_post"], g_next, "ffn_down")
    return x4.reshape(B, L, D), xn


def kernel(x_prompt, x_sample, mem_prompt, mem_sample, mix_norm_pre, mix_norm_post, w_in, ssm_conv_w, ssm_conv_b,
           ssm_a_log, ssm_dt_bias, ssm_d, ssm_norm, na_rpb, w_out, mem_norm_pre, mem_norm_post, mem_norm_kv,
           mem_wq, mem_wk, mem_wv, mem_wo, ffn_norm_pre, ffn_norm_post, ffn_w_up, ffn_conv_w, ffn_w_down):
    p = dict(mix_norm_pre=mix_norm_pre, mix_norm_post=mix_norm_post, w_in=w_in, ssm_conv_w=ssm_conv_w,
             ssm_conv_b=ssm_conv_b, ssm_a_log=ssm_a_log, ssm_dt_bias=ssm_dt_bias, ssm_d=ssm_d, ssm_norm=ssm_norm,
             na_rpb=na_rpb, w_out=w_out, mem_norm_pre=mem_norm_pre, mem_norm_post=mem_norm_post,
             mem_norm_kv=mem_norm_kv, mem_wq=mem_wq, mem_wk=mem_wk, mem_wv=mem_wv, mem_wo=mem_wo,
             ffn_norm_pre=ffn_norm_pre, ffn_norm_post=ffn_norm_post, ffn_w_up=ffn_w_up, ffn_conv_w=ffn_conv_w,
             ffn_w_down=ffn_w_down)
    groups = [(x_prompt, mem_prompt), (x_sample, mem_sample)]
    tabs = [rotary_table(x.shape[1]) for x, _ in groups]
    xs = [x for x, _ in groups]
    xns = [rmsnorm(x.reshape(-1, D_MODEL), mix_norm_pre[0]) for x in xs]
    for l in range(DEPTH):
        lp = prepare_layer_params(p, l)
        g_next = mix_norm_pre[l + 1] if l + 1 < DEPTH else jnp.ones((D_MODEL,), F32)
        for gi, (_, mem) in enumerate(groups):
            xs[gi], xns[gi] = encoder_layer(xs[gi], xns[gi], mem, lp, g_next, tabs[gi])
    return (xs[0], xs[1])
```

```python
import functools
import math

import jax
import jax.numpy as jnp
import numpy as np
from jax import lax
from jax.experimental import pallas as pl
from jax.experimental.pallas import tpu as pltpu

F32 = jnp.float32
BF16 = jnp.bfloat16

D_MODEL = 4096
DEPTH = 4
HEAD_DIM = 128
A_WIDTH = 1536
SSM_WIDTH = 1536
NA_WIDTH = 1024
DILATED_GROUPS = ((128, 1), (512, 4), (2048, 16))
A_GROUP_HEADS = 4
A_OUT = A_GROUP_HEADS * HEAD_DIM
ROT_DIM = 32
ROPE_THETA = 500000.0
SSM_HEAD_DIM = 64
SSM_HEADS = 24
SSM_GROUPS = 8
SSM_GROUP_HEADS = SSM_HEADS // SSM_GROUPS
SSM_STATE = 128
SSM_CONV = 5
SSM_CHUNK = 128
NA_HEADS = 8
GRID_W = 64
NA_ROWS = 8
NA_COLS = 16
NA_KEY_ROWS = 10
MEM_TOKENS = 256
MEM_HEADS = 4
MEM_WIDTH = MEM_HEADS * HEAD_DIM
D_FF = 11008
D_FF_PAD = 11264
EPS = 1e-6

COL_A_GROUP = 3 * A_OUT
COL_Z, COL_X, COL_DT = 4608, 6144, 7680
COL_CQ, COL_CK, COL_CV, COL_B, COL_C = 8192, 9216, 10240, 11264, 12288
IN_COLS_PAD = 13312
DT_LANES = 128

VMEM_LIMIT = 56 * 1024 * 1024
NEG_BIG = -1e30


def _cparams(*sem):
    return pltpu.CompilerParams(dimension_semantics=sem, vmem_limit_bytes=VMEM_LIMIT)


def _rms(x, g):
    ms = jnp.mean(x * x, axis=-1, keepdims=True)
    return x * lax.rsqrt(ms + EPS) * g


def _rmsnorm_body(x_ref, g_ref, o_ref):
    o_ref[...] = _rms(x_ref[...], g_ref[...]).astype(o_ref.dtype)


def rmsnorm(x, g, tm=256):
    T, D = x.shape
    return pl.pallas_call(
        _rmsnorm_body,
        grid=(T // tm,),
        in_specs=[pl.BlockSpec((tm, D), lambda i: (i, 0)), pl.BlockSpec((1, D), lambda i: (0, 0))],
        out_specs=pl.BlockSpec((tm, D), lambda i: (i, 0)),
        out_shape=jax.ShapeDtypeStruct((T, D), BF16),
        compiler_params=_cparams("parallel"),
        name="rmsnorm",
    )(x, g.reshape(1, D))


def _mm_body(x_ref, w_ref, o_ref):
    o_ref[...] = jnp.dot(x_ref[...], w_ref[...], preferred_element_type=F32).astype(o_ref.dtype)


def matmul(x, w, tm, tn, name):
    T, K = x.shape
    N = w.shape[1]
    return pl.pallas_call(
        _mm_body,
        grid=(T // tm, N // tn),
        in_specs=[pl.BlockSpec((tm, K), lambda i, j: (i, 0)), pl.BlockSpec((K, tn), lambda i, j: (0, j))],
        out_specs=pl.BlockSpec((tm, tn), lambda i, j: (i, j)),
        out_shape=jax.ShapeDtypeStruct((T, N), BF16),
        compiler_params=_cparams("parallel", "arbitrary"),
        name=name,
    )(x, w)


ROW_SLAB = 128
COL_SLAB = 1024


def _accumulate(xo_ref, lhs, w_ref, first=False):
    for c in range(0, xo_ref.shape[1], COL_SLAB):
        cols = slice(c, c + COL_SLAB)
        part = jnp.dot(lhs, w_ref[:, cols], preferred_element_type=F32)
        if first:
            xo_ref[:, cols] = part
        else:
            xo_ref[:, cols] += part


def _residual_rows(rows, xo_ref, x_ref, gp_ref, gn_ref, xn_ref):
    x_new = x_ref[rows, :] + _rms(xo_ref[rows, :], gp_ref[...])
    xo_ref[rows, :] = x_new
    xn_ref[rows, :] = _rms(x_new, gn_ref[...]).astype(xn_ref.dtype)


def _accumulate_and_finish(xo_ref, lhs_ref, w_ref, x_ref, gp_ref, gn_ref, xn_ref, first=False):
    slabs = [slice(r, r + ROW_SLAB) for r in range(0, xo_ref.shape[0], ROW_SLAB)]

    def matmul_rows(rows):
        part = jnp.dot(lhs_ref[rows, :], w_ref[...], preferred_element_type=F32)
        if first:
            xo_ref[rows, :] = part
        else:
            xo_ref[rows, :] += part

    matmul_rows(slabs[0])
    for i, rows in enumerate(slabs):
        if i + 1 < len(slabs):
            matmul_rows(slabs[i + 1])
        _residual_rows(rows, xo_ref, x_ref, gp_ref, gn_ref, xn_ref)


def _out_proj_body(oa_ref, ob_ref, oc_ref, w_ref, x_ref, gp_ref, gn_ref, xo_ref, xn_ref, *, ka, kb):
    k = pl.program_id(1)

    @pl.when(k == 0)
    def _():
        _accumulate(xo_ref, oa_ref[...], w_ref, first=True)

    @pl.when((k > 0) & (k < ka))
    def _():
        _accumulate(xo_ref, oa_ref[...], w_ref)

    @pl.when((k >= ka) & (k < ka + kb))
    def _():
        _accumulate(xo_ref, ob_ref[...], w_ref)

    last = pl.num_programs(1) - 1

    @pl.when((k >= ka + kb) & (k < last))
    def _():
        _accumulate(xo_ref, oc_ref[...], w_ref)

    @pl.when(k == last)
    def _():
        _accumulate_and_finish(xo_ref, oc_ref, w_ref, x_ref, gp_ref, gn_ref, xn_ref)


def mixer_out_proj(o_a, o_b, o_c, w, x, g_post, g_next, tm=512, tk=512):
    T = x.shape[0]
    D = w.shape[1]
    ka, kb, kc = o_a.shape[1] // tk, o_b.shape[1] // tk, o_c.shape[1] // tk
    row = lambda i, k: (i, 0)
    return pl.pallas_call(
        functools.partial(_out_proj_body, ka=ka, kb=kb),
        grid=(T // tm, ka + kb + kc),
        in_specs=[
            pl.BlockSpec((tm, tk), lambda i, k: (i, jnp.clip(k, 0, ka - 1))),
            pl.BlockSpec((tm, tk), lambda i, k: (i, jnp.clip(k - ka, 0, kb - 1))),
            pl.BlockSpec((tm, tk), lambda i, k: (i, jnp.clip(k - ka - kb, 0, kc - 1))),
            pl.BlockSpec((tk, D), lambda i, k: (k, 0)),
            pl.BlockSpec((tm, D), row),
            pl.BlockSpec((1, D), lambda i, k: (0, 0)),
            pl.BlockSpec((1, D), lambda i, k: (0, 0)),
        ],
        out_specs=[pl.BlockSpec((tm, D), row), pl.BlockSpec((tm, D), row)],
        out_shape=[jax.ShapeDtypeStruct((T, D), F32), jax.ShapeDtypeStruct((T, D), BF16)],
        compiler_params=_cparams("parallel", "arbitrary"),
        name="out_proj",
    )(o_a, o_b, o_c, w, x, g_post.reshape(1, D), g_next.reshape(1, D))


def rotary_table(L):
    half = ROT_DIM // 2
    inv_freq = jnp.exp(-math.log(ROPE_THETA) * jnp.arange(half, dtype=F32) / half)
    ang = jnp.arange(L, dtype=F32)[:, None] * inv_freq[None, :]
    cos, sin = jnp.cos(ang), jnp.sin(ang)
    ones = jnp.ones((L, HEAD_DIM - ROT_DIM), F32)
    zeros = jnp.zeros((L, HEAD_DIM - ROT_DIM), F32)
    return jnp.concatenate([cos, cos, ones, -sin, sin, zeros], axis=1)


def _rope(t, tab):
    lane = lax.broadcasted_iota(jnp.int32, t.shape, 1)
    partner = jnp.where(lane < ROT_DIM // 2, pltpu.roll(t, HEAD_DIM - ROT_DIM // 2, 1), pltpu.roll(t, ROT_DIM // 2, 1))
    return t * tab[:, :HEAD_DIM] + partner * tab[:, HEAD_DIM:]


def _dilated_body(q_ref, kp_ref, kc_ref, kn_ref, vp_ref, vc_ref, vn_ref, tp_ref, tc_ref, tn_ref,
                  o_ref, lse_ref, *, radius, m_len):
    m = pl.program_id(2)
    tq = q_ref.shape[0]
    row = lax.broadcasted_iota(jnp.int32, (tq, 3 * tq), 0)
    col = lax.broadcasted_iota(jnp.int32, (tq, 3 * tq), 1)
    kpos = (m - 1) * tq + col
    rel = col - tq - row
    valid = (jnp.abs(rel) <= radius) & (kpos >= 0) & (kpos < m_len)
    scale = HEAD_DIM ** -0.5
    nt = (((1,), (1,)), ((), ()))

    def scores(h):
        hs = slice(h * HEAD_DIM, (h + 1) * HEAD_DIM)
        q = _rope(q_ref[:, hs].astype(F32), tc_ref[...]).astype(BF16)
        parts = []
        for k_ref, t_ref in ((kp_ref, tp_ref), (kc_ref, tc_ref), (kn_ref, tn_ref)):
            kk = _rope(k_ref[:, hs].astype(F32), t_ref[...]).astype(BF16)
            parts.append(lax.dot_general(q, kk, nt, preferred_element_type=F32))
        return jnp.where(valid, jnp.concatenate(parts, axis=1) * scale, NEG_BIG)

    s_next = scores(0)
    for h in range(A_GROUP_HEADS):
        hs = slice(h * HEAD_DIM, (h + 1) * HEAD_DIM)
        s = s_next
        if h + 1 < A_GROUP_HEADS:
            s_next = scores(h + 1)
        mx = jnp.max(s, axis=-1, keepdims=True)
        p = jnp.exp(s - mx)
        l = jnp.sum(p, axis=-1, keepdims=True)
        pb = p.astype(BF16)
        o = jnp.dot(pb[:, :tq], vp_ref[:, hs], preferred_element_type=F32)
        o += jnp.dot(pb[:, tq:2 * tq], vc_ref[:, hs], preferred_element_type=F32)
        o += jnp.dot(pb[:, 2 * tq:], vn_ref[:, hs], preferred_element_type=F32)
        o_ref[:, hs] = o / l
        lse_ref[:, hs] = jnp.broadcast_to(mx + jnp.log(l), (tq, HEAD_DIM))


def dilated_attention(proj, tab, g, dil, radius):
    B, L, C = proj.shape
    M = L // dil
    tq = 128
    nmb = M // tq
    if dil == 1:
        pv, ncb, cq = proj, C // A_OUT, g * 3
    else:
        pv = proj[:, :, g * COL_A_GROUP:(g + 1) * COL_A_GROUP].reshape(B, M, dil * COL_A_GROUP)
        ncb, cq = 3, 0
    ck, cv = cq + 1, cq + 2
    tv = tab.reshape(M, dil * 2 * HEAD_DIM)

    def blk(off, col0):
        return pl.BlockSpec((None, tq, A_OUT),
                            lambda b, r, m: (b, jnp.clip(m + off, 0, nmb - 1), r * ncb + col0))

    def tblk(off):
        return pl.BlockSpec((tq, 2 * HEAD_DIM), lambda b, r, m: (jnp.clip(m + off, 0, nmb - 1), r))

    out_spec = pl.BlockSpec((None, tq, A_OUT), lambda b, r, m: (b, m, r))
    o, lse = pl.pallas_call(
        functools.partial(_dilated_body, radius=radius, m_len=M),
        grid=(B, dil, nmb),
        in_specs=[blk(0, cq), blk(-1, ck), blk(0, ck), blk(1, ck), blk(-1, cv), blk(0, cv), blk(1, cv),
                  tblk(-1), tblk(0), tblk(1)],
        out_specs=[out_spec, out_spec],
        out_shape=[jax.ShapeDtypeStruct((B, M, dil * A_OUT), F32)] * 2,
        compiler_params=_cparams("parallel", "parallel", "arbitrary"),
        name=f"dilated_attn_d{dil}",
    )(pv, pv, pv, pv, pv, pv, pv, tv, tv, tv)
    return o.reshape(B * L, A_OUT), lse.reshape(B * L, A_OUT)


def _combine_body(o0, o1, o2, l0, l1, l2, out_ref):
    a, b, c = l0[...], l1[...], l2[...]
    mx = jnp.maximum(jnp.maximum(a, b), c)
    ea, eb, ec = jnp.exp(a - mx), jnp.exp(b - mx), jnp.exp(c - mx)
    den = ea + eb + ec
    out_ref[...] = ((ea * o0[...] + eb * o1[...] + ec * o2[...]) / den).astype(out_ref.dtype)


def combine_groups(outs, lses, tm=512):
    T, W = outs[0].shape
    spec = pl.BlockSpec((tm, W), lambda i: (i, 0))
    return pl.pallas_call(
        _combine_body,
        grid=(T // tm,),
        in_specs=[spec] * 6,
        out_specs=spec,
        out_shape=jax.ShapeDtypeStruct((T, W), BF16),
        compiler_params=_cparams("parallel"),
        name="dilated_combine",
    )(*outs, *lses)


def _shift_rows(x, prev8, next8, s):
    if s == 0:
        return x
    n = x.shape[0]
    rolled = pltpu.roll(x, s % n, 0)
    row8 = lax.broadcasted_iota(jnp.int32, prev8.shape, 0)
    if s > 0:
        head = jnp.where(row8 < s, pltpu.roll(prev8, s, 0), rolled[:8])
        return jnp.concatenate([head, rolled[8:]], axis=0)
    tail = jnp.where(row8 >= 8 + s, pltpu.roll(next8, s % 8, 0), rolled[n - 8:])
    return jnp.concatenate([rolled[:n - 8], tail], axis=0)


def _conv_silu_body(x_ref, xp_ref, xn_ref, w_ref, b_ref, o_ref, *, halo):
    i = pl.program_id(1)
    tl = x_ref.shape[0]
    prev8 = jnp.where(i == 0, 0.0, xp_ref[halo - 8:, :].astype(F32))
    next8 = jnp.where(i == pl.num_programs(1) - 1, 0.0, xn_ref[:8, :].astype(F32))
    x = x_ref[...].astype(F32)
    acc = jnp.broadcast_to(b_ref[...], (tl, b_ref.shape[1]))
    pad = SSM_CONV // 2
    for k in range(SSM_CONV):
        acc = acc + w_ref[k:k + 1, :] * _shift_rows(x, prev8, next8, pad - k)
    o_ref[...] = (acc / (1.0 + jnp.exp(-acc))).astype(o_ref.dtype)


def conv_silu(proj, col0, w, b, tl=512, tc=512):
    B, L, _ = proj.shape
    C = w.shape[1]
    halo = 16
    cb0 = col0 // tc
    hb = tl // halo
    nhb = L // halo
    return pl.pallas_call(
        functools.partial(_conv_silu_body, halo=halo),
        grid=(B, L // tl, C // tc),
        in_specs=[
            pl.BlockSpec((None, tl, tc), lambda b, i, j: (b, i, cb0 + j)),
            pl.BlockSpec((None, halo, tc), lambda b, i, j: (b, jnp.maximum(i * hb - 1, 0), cb0 + j)),
            pl.BlockSpec((None, halo, tc), lambda b, i, j: (b, jnp.minimum((i + 1) * hb, nhb - 1), cb0 + j)),
            pl.BlockSpec((SSM_CONV, tc), lambda b, i, j: (0, j)),
            pl.BlockSpec((1, tc), lambda b, i, j: (0, j)),
        ],
        out_specs=pl.BlockSpec((None, tl, tc), lambda b, i, j: (b, i, j)),
        out_shape=jax.ShapeDtypeStruct((B, L, C), BF16),
        compiler_params=_cparams("parallel", "parallel", "arbitrary"),
        name="ssm_conv_silu",
    )(proj, proj, proj, w, b.reshape(1, C))


def _split3(a):
    hi = a.astype(BF16)
    r1 = a - hi.astype(F32)
    mid = r1.astype(BF16)
    lo = (r1 - mid.astype(F32)).astype(BF16)
    return jnp.concatenate([hi, mid, lo], axis=1)


def _ssd_chunk_terms(dt_ref, bias_ref, alog_ref):
    Q = SSM_CHUNK
    lane = lax.broadcasted_iota(jnp.int32, (Q, DT_LANES), 1)
    row = lax.broadcasted_iota(jnp.int32, (Q, DT_LANES), 0)
    v = dt_ref[...].astype(F32) + bias_ref[...]
    dt = jnp.maximum(v, 0.0) + jnp.log1p(jnp.exp(-jnp.abs(v)))
    a_row = -jnp.exp(alog_ref[...])
    da = jnp.where(lane < 2 * SSM_HEADS, dt * a_row, 0.0)
    pre, suf = da, da
    s = 1
    while s < Q:
        pre = pre + jnp.where(row >= s, pltpu.roll(pre, s, 0), 0.0)
        suf = suf + jnp.where(row < Q - s, pltpu.roll(suf, Q - s, 0), 0.0)
        s *= 2
    is_fwd = lane < SSM_HEADS
    cs = jnp.where(is_fwd, pre, suf)
    tot = jnp.where(is_fwd[:1], cs[Q - 1:Q, :], cs[0:1, :])
    e = jnp.exp(cs)
    w = jnp.exp(tot - cs) * dt
    return dt, cs, e, w


def _ssd_fwd_body(x_ref, bc_ref, dt_ref, bias_ref, alog_ref, d_ref, exp_ref, y_ref, h_ref):
    Q, G, R, P, N = SSM_CHUNK, SSM_GROUPS, SSM_GROUP_HEADS, SSM_HEAD_DIM, SSM_STATE
    GP = R * P

    @pl.when(pl.program_id(1) == 0)
    def _():
        h_ref[...] = jnp.zeros_like(h_ref)

    dt, cs, e, w = _ssd_chunk_terms(dt_ref, bias_ref, alog_ref)
    cs_t = cs.T
    dt_t = dt.T
    e_x = jnp.dot(_split3(e), exp_ref[:, :SSM_WIDTH], preferred_element_type=F32)
    w_x = jnp.dot(_split3(w), exp_ref[:, :SSM_WIDTH], preferred_element_type=F32)
    x = x_ref[...]
    xf = x.astype(F32)
    xw = (xf * w_x).astype(BF16)
    tri = lax.broadcasted_iota(jnp.int32, (Q, Q), 0) >= lax.broadcasted_iota(jnp.int32, (Q, Q), 1)
    nt = (((1,), (1,)), ((), ()))
    cbs, y_offs = [], []
    for g in range(G):
        bg = bc_ref[:, g * N:(g + 1) * N]
        cg = bc_ref[:, G * N + g * N:G * N + (g + 1) * N]
        cbs.append(lax.dot_general(cg, bg, nt, preferred_element_type=F32))
        gs = slice(g * GP, (g + 1) * GP)
        hg = h_ref[g]
        y_offs.append(jnp.dot(cg, hg.astype(BF16), preferred_element_type=F32))
        bg_t = bg.astype(F32).T.astype(BF16)
        h_ref[g] = hg * e_x[Q - 1:Q, gs] + jnp.dot(bg_t, xw[:, gs], preferred_element_type=F32)
    ys = []
    for h in range(SSM_HEADS):
        hb = SSM_HEADS + h
        arg = jnp.where(tri, jnp.broadcast_to(cs[:, h:h + 1], (Q, Q)) - cs_t[h:h + 1, :],
                        jnp.broadcast_to(cs[:, hb:hb + 1], (Q, Q)) - cs_t[hb:hb + 1, :])
        dtrow = jnp.where(tri, dt_t[h:h + 1, :], dt_t[hb:hb + 1, :])
        wm = (cbs[h // R] * jnp.exp(arg) * dtrow).astype(BF16)
        ys.append(jnp.dot(wm, x[:, h * P:(h + 1) * P], preferred_element_type=F32))
    y_ref[...] = jnp.concatenate(ys, axis=1) + jnp.concatenate(y_offs, axis=1) * e_x + d_ref[...] * xf


def _ssd_bwd_body(x_ref, bc_ref, dt_ref, bias_ref, alog_ref, exp_ref, yf_ref, z_ref, gain_ref, o_ref, h_ref):
    Q, G, R, P, N = SSM_CHUNK, SSM_GROUPS, SSM_GROUP_HEADS, SSM_HEAD_DIM, SSM_STATE
    GP = R * P

    @pl.when(pl.program_id(1) == 0)
    def _():
        h_ref[...] = jnp.zeros_like(h_ref)

    _, _, e, w = _ssd_chunk_terms(dt_ref, bias_ref, alog_ref)
    e_x = jnp.dot(_split3(e), exp_ref[:, SSM_WIDTH:], preferred_element_type=F32)
    w_x = jnp.dot(_split3(w), exp_ref[:, SSM_WIDTH:], preferred_element_type=F32)
    xw = (x_ref[...].astype(F32) * w_x).astype(BF16)
    ys = []
    for g in range(G):
        bg = bc_ref[:, g * N:(g + 1) * N]
        cg = bc_ref[:, G * N + g * N:G * N + (g + 1) * N]
        gs = slice(g * GP, (g + 1) * GP)
        hg = h_ref[g]
        ys.append(jnp.dot(cg, hg.astype(BF16), preferred_element_type=F32))
        bg_t = bg.astype(F32).T.astype(BF16)
        h_ref[g] = hg * e_x[0:1, gs] + jnp.dot(bg_t, xw[:, gs], preferred_element_type=F32)
    y = yf_ref[...] + jnp.concatenate(ys, axis=1) * e_x
    z = z_ref[...].astype(F32)
    y = y * (z / (1.0 + jnp.exp(-z)))
    o_ref[...] = _rms(y, gain_ref[...]).astype(o_ref.dtype)


def _ssd_expand_matrix():
    ex = np.zeros((DT_LANES, 2 * SSM_WIDTH), np.float32)
    for h in range(2 * SSM_HEADS):
        ex[h, h * SSM_HEAD_DIM:(h + 1) * SSM_HEAD_DIM] = 1.0
    return jnp.asarray(np.tile(ex, (3, 1)), BF16)


def ssd_mixer(proj, xs, bc, dt_bias, a_log, d_row, gain):
    B, L, _ = proj.shape
    Q = SSM_CHUNK
    nc = L // Q
    ex = _ssd_expand_matrix()
    full = lambda shape: pl.BlockSpec(shape, lambda b, c: (0,) * len(shape))
    state = pltpu.VMEM((SSM_GROUPS, SSM_STATE, SSM_GROUP_HEADS * SSM_HEAD_DIM), F32)

    def chunk(width, col_block, rev):
        if rev:
            return pl.BlockSpec((None, Q, width), lambda b, c: (b, nc - 1 - c, col_block))
        return pl.BlockSpec((None, Q, width), lambda b, c: (b, c, col_block))

    y_f = pl.pallas_call(
        _ssd_fwd_body,
        grid=(B, nc),
        in_specs=[chunk(SSM_WIDTH, 0, False), chunk(2 * SSM_GROUPS * SSM_STATE, 0, False),
                  chunk(DT_LANES, COL_DT // DT_LANES, False),
                  full((1, DT_LANES)), full((1, DT_LANES)), full((1, SSM_WIDTH)), full(ex.shape)],
        out_specs=chunk(SSM_WIDTH, 0, False),
        out_shape=jax.ShapeDtypeStruct((B, L, SSM_WIDTH), F32),
        scratch_shapes=[state],
        compiler_params=_cparams("parallel", "arbitrary"),
        name="ssd_fwd",
    )(xs, bc, proj, dt_bias, a_log, d_row, ex)
    o_b = pl.pallas_call(
        _ssd_bwd_body,
        grid=(B, nc),
        in_specs=[chunk(SSM_WIDTH, 0, True), chunk(2 * SSM_GROUPS * SSM_STATE, 0, True),
                  chunk(DT_LANES, COL_DT // DT_LANES, True),
                  full((1, DT_LANES)), full((1, DT_LANES)), full(ex.shape),
                  chunk(SSM_WIDTH, 0, True), chunk(SSM_WIDTH, COL_Z // SSM_WIDTH, True), full((1, SSM_WIDTH))],
        out_specs=chunk(SSM_WIDTH, 0, True),
        out_shape=jax.ShapeDtypeStruct((B, L, SSM_WIDTH), BF16),
        scratch_shapes=[state],
        compiler_params=_cparams("parallel", "arbitrary"),
        name="ssd_bwd",
    )(xs, bc, proj, dt_bias, a_log, ex, y_f, proj, gain)
    return o_b.reshape(B * L, SSM_WIDTH)


def na_bias_table(rpb, rows):
    nrb = rows // 2
    ii = np.array([0, 1, 2, nrb - 2, nrb - 1])
    qrow = ii[:, None] * 2 + np.arange(2)[None, :]
    kstart = np.clip(ii * 2 - NA_ROWS // 2, 0, rows - NA_KEY_ROWS)
    krow = kstart[:, None] + np.arange(NA_KEY_ROWS)[None, :]
    wr = np.clip(qrow - NA_ROWS // 2, 0, rows - NA_ROWS)
    row_ok = (krow[:, None, :] >= wr[:, :, None]) & (krow[:, None, :] < wr[:, :, None] + NA_ROWS)
    row_off = np.clip(krow[:, None, :] - qrow[:, :, None] + NA_ROWS - 1, 0, 2 * NA_ROWS - 2)
    qcol = np.arange(GRID_W)
    wc = np.clip(qcol - NA_COLS // 2, 0, GRID_W - NA_COLS)
    col_ok = (qcol[None, :] >= wc[:, None]) & (qcol[None, :] < wc[:, None] + NA_COLS)
    col_off = np.clip(qcol[None, :] - qcol[:, None] + NA_COLS - 1, 0, 2 * NA_COLS - 2)
    row_sel = (row_off[..., None] == np.arange(2 * NA_ROWS - 1)).astype(np.float32)
    col_sel = (col_off[..., None] == np.arange(2 * NA_COLS - 1)).astype(np.float32)
    hi = lax.Precision.HIGHEST
    t1 = jnp.einsum('pqkr,hrc->phqkc', row_sel, rpb.astype(F32), precision=hi)
    bias = jnp.einsum('phqkc,xyc->phqxky', t1, col_sel, precision=hi)
    mask = row_ok[:, None, :, None, :, None] & col_ok[None, None, None, :, None, :]
    tab = jnp.where(mask, bias, NEG_BIG)
    return tab.reshape(5, rpb.shape[0], 2 * GRID_W, NA_KEY_ROWS * GRID_W)


def _na_body(q_ref, k0, k1, k2, k3, k4, v0, v1, v2, v3, v4, bias_ref, o_ref):
    scale = HEAD_DIM ** -0.5
    nt = (((1,), (1,)), ((), ()))
    k_refs = (k0, k1, k2, k3, k4)
    v_refs = (v0, v1, v2, v3, v4)
    tq = q_ref.shape[0]

    def scores(h):
        hs = slice(h * HEAD_DIM, (h + 1) * HEAD_DIM)
        q = q_ref[:, hs]
        s = jnp.concatenate([lax.dot_general(q, kr[:, hs], nt, preferred_element_type=F32) for kr in k_refs], axis=1)
        return s * scale + bias_ref[h]

    s_next = scores(0)
    for h in range(NA_HEADS):
        hs = slice(h * HEAD_DIM, (h + 1) * HEAD_DIM)
        s = s_next
        if h + 1 < NA_HEADS:
            s_next = scores(h + 1)
        mx = jnp.max(s, axis=-1, keepdims=True)
        p = jnp.exp(s - mx)
        l = jnp.sum(p, axis=-1, keepdims=True)
        pb = p.astype(BF16)
        o = jnp.dot(pb[:, :tq], v_refs[0][:, hs], preferred_element_type=F32)
        for j in range(1, 5):
            o += jnp.dot(pb[:, j * tq:(j + 1) * tq], v_refs[j][:, hs], preferred_element_type=F32)
        o_ref[:, hs] = (o / l).astype(o_ref.dtype)


def neighbourhood_attention(proj, bias_tab):
    B, L, _ = proj.shape
    tq = 2 * GRID_W
    nrb = L // tq
    W = NA_WIDTH

    def pattern(i):
        return jnp.where(i < 2, i, jnp.where(i >= nrb - 2, i - (nrb - 5), 2))

    def kv(j, col):
        return pl.BlockSpec((None, tq, W), lambda b, i: (b, jnp.clip(i - 2, 0, nrb - 5) + j, col))

    cq, ck, cv = COL_CQ // W, COL_CK // W, COL_CV // W
    o = pl.pallas_call(
        _na_body,
        grid=(B, nrb),
        in_specs=[pl.BlockSpec((None, tq, W), lambda b, i: (b, i, cq))]
        + [kv(j, ck) for j in range(5)] + [kv(j, cv) for j in range(5)]
        + [pl.BlockSpec((None, NA_HEADS, tq, 5 * tq), lambda b, i: (pattern(i), 0, 0, 0))],
        out_specs=pl.BlockSpec((None, tq, W), lambda b, i: (b, i, 0)),
        out_shape=jax.ShapeDtypeStruct((B, L, W), BF16),
        compiler_params=_cparams("parallel", "arbitrary"),
        name="neighbourhood_attn",
    )(*([proj] * 11), bias_tab)
    return o.reshape(B * L, W)


def _mem_attn_body(xn_ref, x_ref, kv_ref, wq_ref, wo_ref, gp_ref, gn_ref, xo_ref, xnn_ref, o_ref):
    scale = HEAD_DIM ** -0.5
    nt = (((1,), (1,)), ((), ()))
    q = jnp.dot(xn_ref[...], wq_ref[...], preferred_element_type=F32).astype(BF16)
    outs = []

    def scores(h):
        hs = slice(h * HEAD_DIM, (h + 1) * HEAD_DIM)
        return lax.dot_general(q[:, hs], kv_ref[:, hs], nt, preferred_element_type=F32) * scale

    s_next = scores(0)
    for h in range(MEM_HEADS):
        vs = slice(MEM_WIDTH + h * HEAD_DIM, MEM_WIDTH + (h + 1) * HEAD_DIM)
        s = s_next
        if h + 1 < MEM_HEADS:
            s_next = scores(h + 1)
        mx = jnp.max(s, axis=-1, keepdims=True)
        p = jnp.exp(s - mx)
        l = jnp.sum(p, axis=-1, keepdims=True)
        o = jnp.dot(p.astype(BF16), kv_ref[:, vs], preferred_element_type=F32) / l
        outs.append(o.astype(BF16))
    o_ref[...] = jnp.concatenate(outs, axis=1)
    _accumulate_and_finish(xo_ref, o_ref, wo_ref, x_ref, gp_ref, gn_ref, xnn_ref, first=True)


def memory_attention(xn, x, kv, wq, wo, g_post, g_next, tm=256):
    B, L, D = x.shape
    tok = pl.BlockSpec((None, tm, D), lambda b, i: (b, i, 0))
    full = lambda shape: pl.BlockSpec(shape, lambda b, i: (0,) * len(shape))
    return pl.pallas_call(
        _mem_attn_body,
        grid=(B, L // tm),
        in_specs=[tok, tok, pl.BlockSpec((None, MEM_TOKENS, 2 * MEM_WIDTH), lambda b, i: (b, 0, 0)),
                  full((D, MEM_WIDTH)), full((MEM_WIDTH, D)), full((1, D)), full((1, D))],
        out_specs=[tok, tok],
        out_shape=[jax.ShapeDtypeStruct((B, L, D), F32), jax.ShapeDtypeStruct((B, L, D), BF16)],
        scratch_shapes=[pltpu.VMEM((tm, MEM_WIDTH), BF16)],
        compiler_params=_cparams("parallel", "arbitrary"),
        name="memory_attn",
    )(xn, x, kv, wq, wo, g_post.reshape(1, D), g_next.reshape(1, D))


def _ffn_up_body(xn_ref, xp_ref, xnx_ref, wg_ref, wv_ref, cw_ref, o_ref, xe_ref, ext_ref, *, halo, tiles_per_seq):
    i = pl.program_id(0)
    tm = xn_ref.shape[0]

    @pl.when(pl.program_id(1) == 0)
    def _():
        xe_ref[0:halo, :] = xp_ref[...]
        xe_ref[halo:halo + tm, :] = xn_ref[...]
        xe_ref[halo + tm:, :] = xnx_ref[...]

    pos = lax.rem(i, tiles_per_seq)
    gate = jnp.dot(xe_ref[...], wg_ref[...], preferred_element_type=F32)
    ext_ref[0:halo, :] = jnp.where(pos == 0, 0.0, gate[0:halo])
    ext_ref[halo:halo + tm, :] = gate[halo:halo + tm]
    ext_ref[halo + tm:, :] = jnp.where(pos == tiles_per_seq - 1, 0.0, gate[halo + tm:])
    val = jnp.dot(xn_ref[...], wv_ref[...], preferred_element_type=F32)
    conv = (cw_ref[0:1, :] * ext_ref[halo - 1:halo - 1 + tm, :]
            + cw_ref[1:2, :] * ext_ref[halo:halo + tm, :]
            + cw_ref[2:3, :] * ext_ref[halo + 1:halo + 1 + tm, :])
    cdf = 0.5 * (1.0 + jnp.tanh(math.sqrt(2.0 / math.pi) * (conv + 0.044715 * (conv * conv * conv))))
    o_ref[...] = (conv * cdf * val).astype(o_ref.dtype)


def ffn_up_gated(xn, w_up, conv_w, seq_len, tm=1024, tn=512):
    T, D = xn.shape
    F = w_up.shape[1] // 2
    halo = 16
    nj = F // tn
    hb = tm // halo
    nhb = T // halo
    return pl.pallas_call(
        functools.partial(_ffn_up_body, halo=halo, tiles_per_seq=seq_len // tm),
        grid=(T // tm, nj),
        in_specs=[
            pl.BlockSpec((tm, D), lambda i, j: (i, 0)),
            pl.BlockSpec((halo, D), lambda i, j: (jnp.maximum(i * hb - 1, 0), 0)),
            pl.BlockSpec((halo, D), lambda i, j: (jnp.minimum((i + 1) * hb, nhb - 1), 0)),
            pl.BlockSpec((D, tn), lambda i, j: (0, j)),
            pl.BlockSpec((D, tn), lambda i, j: (0, nj + j)),
            pl.BlockSpec((3, tn), lambda i, j: (0, j)),
        ],
        out_specs=pl.BlockSpec((tm, tn), lambda i, j: (i, j)),
        out_shape=jax.ShapeDtypeStruct((T, F), BF16),
        scratch_shapes=[pltpu.VMEM((tm + 2 * halo, D), BF16), pltpu.VMEM((tm + 2 * halo, tn), F32)],
        compiler_params=_cparams("parallel", "arbitrary"),
        name="ffn_up",
    )(xn, xn, xn, w_up, w_up, conv_w)


def _mm_res_body(h_ref, w_ref, x_ref, gp_ref, gn_ref, xo_ref, xn_ref):
    @pl.when(pl.program_id(1) == 0)
    def _():
        _accumulate(xo_ref, h_ref[...], w_ref, first=True)

    last = pl.num_programs(1) - 1

    @pl.when((pl.program_id(1) > 0) & (pl.program_id(1) < last))
    def _():
        _accumulate(xo_ref, h_ref[...], w_ref)

    @pl.when(pl.program_id(1) == last)
    def _():
        _accumulate_and_finish(xo_ref, h_ref, w_ref, x_ref, gp_ref, gn_ref, xn_ref)


def matmul_residual(h, w, x, g_post, g_next, name, tm=512, tk=512):
    T, K = h.shape
    D = w.shape[1]
    row = lambda i, k: (i, 0)
    return pl.pallas_call(
        _mm_res_body,
        grid=(T // tm, K // tk),
        in_specs=[
            pl.BlockSpec((tm, tk), lambda i, k: (i, k)),
            pl.BlockSpec((tk, D), lambda i, k: (k, 0)),
            pl.BlockSpec((tm, D), row),
            pl.BlockSpec((1, D), lambda i, k: (0, 0)),
            pl.BlockSpec((1, D), lambda i, k: (0, 0)),
        ],
        out_specs=[pl.BlockSpec((tm, D), row), pl.BlockSpec((tm, D), row)],
        out_shape=[jax.ShapeDtypeStruct((T, D), F32), jax.ShapeDtypeStruct((T, D), BF16)],
        compiler_params=_cparams("parallel", "arbitrary"),
        name=name,
    )(h, w, x, g_post.reshape(1, D), g_next.reshape(1, D))


def prepare_layer_params(p, l):
    w_in = p["w_in"][l]
    zeros = lambda r, c: jnp.zeros((r, c), w_in.dtype)
    a_end = 3 * A_WIDTH + SSM_WIDTH + SSM_WIDTH
    bc0 = a_end
    dt0 = bc0 + 2 * SSM_GROUPS * SSM_STATE
    c0 = dt0 + 2 * SSM_HEADS
    a_cols = [w_in[:, s * A_WIDTH + g * A_OUT:s * A_WIDTH + (g + 1) * A_OUT]
              for g in range(len(DILATED_GROUPS)) for s in range(3)]
    w_in_p = jnp.concatenate(
        a_cols + [w_in[:, 3 * A_WIDTH:a_end], w_in[:, dt0:c0], zeros(D_MODEL, COL_CQ - COL_DT - 2 * SSM_HEADS),
                  w_in[:, c0:], w_in[:, bc0:dt0]], axis=1).astype(BF16)
    up = p["ffn_w_up"][l]
    fpad = D_FF_PAD - D_FF
    w_up_p = jnp.concatenate([up[:, :D_FF], zeros(D_MODEL, fpad), up[:, D_FF:], zeros(D_MODEL, fpad)], axis=1).astype(BF16)
    w_down_p = jnp.concatenate([p["ffn_w_down"][l], zeros(fpad, D_MODEL)], axis=0).astype(BF16)
    lanes = lambda v: jnp.concatenate([v.reshape(-1), jnp.zeros((DT_LANES - 2 * SSM_HEADS,), F32)]).reshape(1, DT_LANES)
    conv_w = p["ssm_conv_w"][l]
    conv_b = p["ssm_conv_b"][l]
    return dict(
        w_in=w_in_p, w_up=w_up_p, w_down=w_down_p,
        w_out=p["w_out"][l].astype(BF16),
        mem_wq=p["mem_wq"][l].astype(BF16), mem_wo=p["mem_wo"][l].astype(BF16),
        mem_wkv=jnp.concatenate([p["mem_wk"][l], p["mem_wv"][l]], axis=1).astype(BF16),
        ffn_conv_w=jnp.concatenate([p["ffn_conv_w"][l], jnp.zeros((3, fpad), F32)], axis=1),
        conv_w_x=conv_w[:, :SSM_WIDTH], conv_b_x=conv_b[:SSM_WIDTH],
        conv_w_bc=conv_w[:, SSM_WIDTH:], conv_b_bc=conv_b[SSM_WIDTH:],
        dt_bias=lanes(p["ssm_dt_bias"][l]), a_log=lanes(p["ssm_a_log"][l]),
        d_row=jnp.repeat(p["ssm_d"][l], SSM_HEAD_DIM).reshape(1, SSM_WIDTH),
        ssm_norm=p["ssm_norm"][l].reshape(1, SSM_WIDTH),
        na_rpb=p["na_rpb"][l],
        mix_norm_pre=p["mix_norm_pre"][l], mix_norm_post=p["mix_norm_post"][l],
        mem_norm_pre=p["mem_norm_pre"][l], mem_norm_post=p["mem_norm_post"][l], mem_norm_kv=p["mem_norm_kv"][l],
        ffn_norm_pre=p["ffn_norm_pre"][l], ffn_norm_post=p["ffn_norm_post"][l],
    )


def hybrid_mixer(xn, lp, tab, B, L):
    proj = matmul(xn, lp["w_in"], 1024, 1024, "in_proj").reshape(B, L, IN_COLS_PAD)
    outs, lses = [], []
    for g, (window, dil) in enumerate(DILATED_GROUPS):
        o, lse = dilated_attention(proj, tab, g, dil, window // (2 * dil))
        outs.append(o)
        lses.append(lse)
    o_a = combine_groups(outs, lses)
    xs = conv_silu(proj, COL_X, lp["conv_w_x"], lp["conv_b_x"])
    bc = conv_silu(proj, COL_B, lp["conv_w_bc"], lp["conv_b_bc"])
    o_b = ssd_mixer(proj, xs, bc, lp["dt_bias"], lp["a_log"], lp["d_row"], lp["ssm_norm"])
    o_c = neighbourhood_attention(proj, na_bias_table(lp["na_rpb"], L // GRID_W))
    return o_a, o_b, o_c


def encoder_layer(x, xn, mem, lp, g_next, tab):
    B, L, D = x.shape
    T = B * L
    o_a, o_b, o_c = hybrid_mixer(xn, lp, tab, B, L)
    x2, xn = mixer_out_proj(o_a, o_b, o_c, lp["w_out"], x.reshape(T, D), lp["mix_norm_post"], lp["mem_norm_pre"])
    mem_n = rmsnorm(mem.reshape(B * MEM_TOKENS, D), lp["mem_norm_kv"])
    kv = matmul(mem_n, lp["mem_wkv"], 256, 1024, "mem_kv_proj").reshape(B, MEM_TOKENS, 2 * MEM_WIDTH)
    x3, xn = memory_attention(xn.reshape(B, L, D), x2.reshape(B, L, D), kv, lp["mem_wq"], lp["mem_wo"],
                              lp["mem_norm_post"], lp["ffn_norm_pre"])
    act = ffn_up_gated(xn.reshape(T, D), lp["w_up"], lp["ffn_conv_w"], L)
    x4, xn = matmul_residual(act, lp["w_down"], x3.reshape(T, D), lp["ffn_norm_post"], g_next, "ffn_down")
    return x4.reshape(B, L, D), xn


def kernel(x_prompt, x_sample, mem_prompt, mem_sample, mix_norm_pre, mix_norm_post, w_in, ssm_conv_w, ssm_conv_b,
           ssm_a_log, ssm_dt_bias, ssm_d, ssm_norm, na_rpb, w_out, mem_norm_pre, mem_norm_post, mem_norm_kv,
           mem_wq, mem_wk, mem_wv, mem_wo, ffn_norm_pre, ffn_norm_post, ffn_w_up, ffn_conv_w, ffn_w_down):
    p = dict(mix_norm_pre=mix_norm_pre, mix_norm_post=mix_norm_post, w_in=w_in, ssm_conv_w=ssm_conv_w,
             ssm_conv_b=ssm_conv_b, ssm_a_log=ssm_a_log, ssm_dt_bias=ssm_dt_bias, ssm_d=ssm_d, ssm_norm=ssm_norm,
             na_rpb=na_rpb, w_out=w_out, mem_norm_pre=mem_norm_pre, mem_norm_post=mem_norm_post,
             mem_norm_kv=mem_norm_kv, mem_wq=mem_wq, mem_wk=mem_wk, mem_wv=mem_wv, mem_wo=mem_wo,
             ffn_norm_pre=ffn_norm_pre, ffn_norm_post=ffn_norm_post, ffn_w_up=ffn_w_up, ffn_conv_w=ffn_conv_w,
             ffn_w_down=ffn_w_down)
    groups = [(x_prompt, mem_prompt), (x_sample, mem_sample)]
    tabs = [rotary_table(x.shape[1]) for x, _ in groups]
    xs = [x for x, _ in groups]
    xns = [rmsnorm(x.reshape(-1, D_MODEL), mix_norm_pre[0]) for x in xs]
    for l in range(DEPTH):
        lp = prepare_layer_params(p, l)
        g_next = mix_norm_pre[l + 1] if l + 1 < DEPTH else jnp.ones((D_MODEL,), F32)
        for gi, (_, mem) in enumerate(groups):
            xs[gi], xns[gi] = encoder_layer(xs[gi], xns[gi], mem, lp, g_next, tabs[gi])
    return (xs[0], xs[1])
```

```python
import functools
import math

import jax
import jax.numpy as jnp
import numpy as np
from jax import lax
from jax.experimental import pallas as pl
from jax.experimental.pallas import tpu as pltpu

F32 = jnp.float32
BF16 = jnp.bfloat16

D_MODEL = 4096
DEPTH = 4
HEAD_DIM = 128
A_WIDTH = 1536
SSM_WIDTH = 1536
NA_WIDTH = 1024
DILATED_GROUPS = ((128, 1), (512, 4), (2048, 16))
A_GROUP_HEADS = 4
A_OUT = A_GROUP_HEADS * HEAD_DIM
ROT_DIM = 32
ROPE_THETA = 500000.0
SSM_HEAD_DIM = 64
SSM_HEADS = 24
SSM_GROUPS = 8
SSM_GROUP_HEADS = SSM_HEADS // SSM_GROUPS
SSM_STATE = 128
SSM_CONV = 5
SSM_CHUNK = 128
NA_HEADS = 8
GRID_W = 64
NA_ROWS = 8
NA_COLS = 16
NA_KEY_ROWS = 10
MEM_TOKENS = 256
MEM_HEADS = 4
MEM_WIDTH = MEM_HEADS * HEAD_DIM
D_FF = 11008
D_FF_PAD = 11264
EPS = 1e-6

COL_A_GROUP = 3 * A_OUT
COL_Z, COL_X, COL_DT = 4608, 6144, 7680
COL_CQ, COL_CK, COL_CV, COL_B, COL_C = 8192, 9216, 10240, 11264, 12288
IN_COLS_PAD = 13312
DT_LANES = 128

VMEM_LIMIT = 56 * 1024 * 1024
NEG_BIG = -1e30


def _cparams(*sem):
    return pltpu.CompilerParams(dimension_semantics=sem, vmem_limit_bytes=VMEM_LIMIT)


def _rms(x, g):
    ms = jnp.mean(x * x, axis=-1, keepdims=True)
    return x * lax.rsqrt(ms + EPS) * g


def _rmsnorm_body(x_ref, g_ref, o_ref):
    o_ref[...] = _rms(x_ref[...], g_ref[...]).astype(o_ref.dtype)


def rmsnorm(x, g, tm=256):
    T, D = x.shape
    return pl.pallas_call(
        _rmsnorm_body,
        grid=(T // tm,),
        in_specs=[pl.BlockSpec((tm, D), lambda i: (i, 0)), pl.BlockSpec((1, D), lambda i: (0, 0))],
        out_specs=pl.BlockSpec((tm, D), lambda i: (i, 0)),
        out_shape=jax.ShapeDtypeStruct((T, D), BF16),
        compiler_params=_cparams("parallel"),
        name="rmsnorm",
    )(x, g.reshape(1, D))


def _mm_body(x_ref, w_ref, o_ref):
    o_ref[...] = jnp.dot(x_ref[...], w_ref[...], preferred_element_type=F32).astype(o_ref.dtype)


def matmul(x, w, tm, tn, name):
    T, K = x.shape
    N = w.shape[1]
    return pl.pallas_call(
        _mm_body,
        grid=(T // tm, N // tn),
        in_specs=[pl.BlockSpec((tm, K), lambda i, j: (i, 0)), pl.BlockSpec((K, tn), lambda i, j: (0, j))],
        out_specs=pl.BlockSpec((tm, tn), lambda i, j: (i, j)),
        out_shape=jax.ShapeDtypeStruct((T, N), BF16),
        compiler_params=_cparams("parallel", "arbitrary"),
        name=name,
    )(x, w)


ROW_SLAB = 128
COL_SLAB = 1024


def _accumulate(xo_ref, lhs, w_ref, first=False):
    for c in range(0, xo_ref.shape[1], COL_SLAB):
        cols = slice(c, c + COL_SLAB)
        part = jnp.dot(lhs, w_ref[:, cols], preferred_element_type=F32)
        if first:
            xo_ref[:, cols] = part
        else:
            xo_ref[:, cols] += part


def _residual_rows(rows, xo_ref, x_ref, gp_ref, gn_ref, xn_ref):
    x_new = x_ref[rows, :] + _rms(xo_ref[rows, :], gp_ref[...])
    xo_ref[rows, :] = x_new
    xn_ref[rows, :] = _rms(x_new, gn_ref[...]).astype(xn_ref.dtype)


def _accumulate_and_finish(xo_ref, lhs_ref, w_ref, x_ref, gp_ref, gn_ref, xn_ref, first=False):
    slabs = [slice(r, r + ROW_SLAB) for r in range(0, xo_ref.shape[0], ROW_SLAB)]

    def matmul_rows(rows):
        part = jnp.dot(lhs_ref[rows, :], w_ref[...], preferred_element_type=F32)
        if first:
            xo_ref[rows, :] = part
        else:
            xo_ref[rows, :] += part

    matmul_rows(slabs[0])
    for i, rows in enumerate(slabs):
        if i + 1 < len(slabs):
            matmul_rows(slabs[i + 1])
        _residual_rows(rows, xo_ref, x_ref, gp_ref, gn_ref, xn_ref)


def _out_proj_body(oa_ref, ob_ref, oc_ref, w_ref, x_ref, gp_ref, gn_ref, xo_ref, xn_ref):
    ka = oa_ref.shape[1]
    kb = ka + ob_ref.shape[1]
    for c in range(0, xo_ref.shape[1], COL_SLAB):
        cols = slice(c, c + COL_SLAB)
        acc = jnp.dot(oa_ref[...], w_ref[0:ka, cols], preferred_element_type=F32)
        acc += jnp.dot(ob_ref[...], w_ref[ka:kb, cols], preferred_element_type=F32)
        acc += jnp.dot(oc_ref[...], w_ref[kb:, cols], preferred_element_type=F32)
        xo_ref[:, cols] = acc
    for r in range(0, xo_ref.shape[0], ROW_SLAB):
        _residual_rows(slice(r, r + ROW_SLAB), xo_ref, x_ref, gp_ref, gn_ref, xn_ref)


def mixer_out_proj(o_a, o_b, o_c, w, x, g_post, g_next, tm=256):
    T = x.shape[0]
    K, D = w.shape
    row = lambda i: (i, 0)
    const = lambda i: (0, 0)
    return pl.pallas_call(
        _out_proj_body,
        grid=(T // tm,),
        in_specs=[
            pl.BlockSpec((tm, o_a.shape[1]), row),
            pl.BlockSpec((tm, o_b.shape[1]), row),
            pl.BlockSpec((tm, o_c.shape[1]), row),
            pl.BlockSpec((K, D), const, pipeline_mode=pl.Buffered(1)),
            pl.BlockSpec((tm, D), row),
            pl.BlockSpec((1, D), const),
            pl.BlockSpec((1, D), const),
        ],
        out_specs=[pl.BlockSpec((tm, D), row), pl.BlockSpec((tm, D), row)],
        out_shape=[jax.ShapeDtypeStruct((T, D), F32), jax.ShapeDtypeStruct((T, D), BF16)],
        compiler_params=_cparams("arbitrary"),
        name="out_proj",
    )(o_a, o_b, o_c, w, x, g_post.reshape(1, D), g_next.reshape(1, D))


def rotary_table(L):
    half = ROT_DIM // 2
    inv_freq = jnp.exp(-math.log(ROPE_THETA) * jnp.arange(half, dtype=F32) / half)
    ang = jnp.arange(L, dtype=F32)[:, None] * inv_freq[None, :]
    cos, sin = jnp.cos(ang), jnp.sin(ang)
    ones = jnp.ones((L, HEAD_DIM - ROT_DIM), F32)
    zeros = jnp.zeros((L, HEAD_DIM - ROT_DIM), F32)
    return jnp.concatenate([cos, cos, ones, -sin, sin, zeros], axis=1)


def _rope(t, tab):
    lane = lax.broadcasted_iota(jnp.int32, t.shape, 1)
    partner = jnp.where(lane < ROT_DIM // 2, pltpu.roll(t, HEAD_DIM - ROT_DIM // 2, 1), pltpu.roll(t, ROT_DIM // 2, 1))
    return t * tab[:, :HEAD_DIM] + partner * tab[:, HEAD_DIM:]


def _dilated_body(q_ref, kp_ref, kc_ref, kn_ref, vp_ref, vc_ref, vn_ref, tp_ref, tc_ref, tn_ref,
                  o_ref, lse_ref, *, radius, m_len):
    m = pl.program_id(2)
    tq = q_ref.shape[0]
    row = lax.broadcasted_iota(jnp.int32, (tq, 3 * tq), 0)
    col = lax.broadcasted_iota(jnp.int32, (tq, 3 * tq), 1)
    kpos = (m - 1) * tq + col
    rel = col - tq - row
    valid = (jnp.abs(rel) <= radius) & (kpos >= 0) & (kpos < m_len)
    scale = HEAD_DIM ** -0.5
    nt = (((1,), (1,)), ((), ()))

    def scores(h):
        hs = slice(h * HEAD_DIM, (h + 1) * HEAD_DIM)
        q = _rope(q_ref[:, hs].astype(F32), tc_ref[...]).astype(BF16)
        parts = []
        for k_ref, t_ref in ((kp_ref, tp_ref), (kc_ref, tc_ref), (kn_ref, tn_ref)):
            kk = _rope(k_ref[:, hs].astype(F32), t_ref[...]).astype(BF16)
            parts.append(lax.dot_general(q, kk, nt, preferred_element_type=F32))
        return jnp.where(valid, jnp.concatenate(parts, axis=1) * scale, NEG_BIG)

    s_next = scores(0)
    for h in range(A_GROUP_HEADS):
        hs = slice(h * HEAD_DIM, (h + 1) * HEAD_DIM)
        s = s_next
        if h + 1 < A_GROUP_HEADS:
            s_next = scores(h + 1)
        mx = jnp.max(s, axis=-1, keepdims=True)
        p = jnp.exp(s - mx)
        l = jnp.sum(p, axis=-1, keepdims=True)
        pb = p.astype(BF16)
        o = jnp.dot(pb[:, :tq], vp_ref[:, hs], preferred_element_type=F32)
        o += jnp.dot(pb[:, tq:2 * tq], vc_ref[:, hs], preferred_element_type=F32)
        o += jnp.dot(pb[:, 2 * tq:], vn_ref[:, hs], preferred_element_type=F32)
        o_ref[:, hs] = o / l
        lse_ref[:, hs] = jnp.broadcast_to(mx + jnp.log(l), (tq, HEAD_DIM))


def dilated_attention(proj, tab, g, dil, radius):
    B, L, C = proj.shape
    M = L // dil
    tq = 128
    nmb = M // tq
    if dil == 1:
        pv, ncb, cq = proj, C // A_OUT, g * 3
    else:
        pv = proj[:, :, g * COL_A_GROUP:(g + 1) * COL_A_GROUP].reshape(B, M, dil * COL_A_GROUP)
        ncb, cq = 3, 0
    ck, cv = cq + 1, cq + 2
    tv = tab.reshape(M, dil * 2 * HEAD_DIM)

    def blk(off, col0):
        return pl.BlockSpec((None, tq, A_OUT),
                            lambda b, r, m: (b, jnp.clip(m + off, 0, nmb - 1), r * ncb + col0))

    def tblk(off):
        return pl.BlockSpec((tq, 2 * HEAD_DIM), lambda b, r, m: (jnp.clip(m + off, 0, nmb - 1), r))

    out_spec = pl.BlockSpec((None, tq, A_OUT), lambda b, r, m: (b, m, r))
    o, lse = pl.pallas_call(
        functools.partial(_dilated_body, radius=radius, m_len=M),
        grid=(B, dil, nmb),
        in_specs=[blk(0, cq), blk(-1, ck), blk(0, ck), blk(1, ck), blk(-1, cv), blk(0, cv), blk(1, cv),
                  tblk(-1), tblk(0), tblk(1)],
        out_specs=[out_spec, out_spec],
        out_shape=[jax.ShapeDtypeStruct((B, M, dil * A_OUT), F32)] * 2,
        compiler_params=_cparams("parallel", "parallel", "arbitrary"),
        name=f"dilated_attn_d{dil}",
    )(pv, pv, pv, pv, pv, pv, pv, tv, tv, tv)
    return o.reshape(B * L, A_OUT), lse.reshape(B * L, A_OUT)


def _combine_body(o0, o1, o2, l0, l1, l2, out_ref):
    a, b, c = l0[...], l1[...], l2[...]
    mx = jnp.maximum(jnp.maximum(a, b), c)
    ea, eb, ec = jnp.exp(a - mx), jnp.exp(b - mx), jnp.exp(c - mx)
    den = ea + eb + ec
    out_ref[...] = ((ea * o0[...] + eb * o1[...] + ec * o2[...]) / den).astype(out_ref.dtype)


def combine_groups(outs, lses, tm=512):
    T, W = outs[0].shape
    spec = pl.BlockSpec((tm, W), lambda i: (i, 0))
    return pl.pallas_call(
        _combine_body,
        grid=(T // tm,),
        in_specs=[spec] * 6,
        out_specs=spec,
        out_shape=jax.ShapeDtypeStruct((T, W), BF16),
        compiler_params=_cparams("parallel"),
        name="dilated_combine",
    )(*outs, *lses)


def _shift_rows(x, prev8, next8, s):
    if s == 0:
        return x
    n = x.shape[0]
    rolled = pltpu.roll(x, s % n, 0)
    row8 = lax.broadcasted_iota(jnp.int32, prev8.shape, 0)
    if s > 0:
        head = jnp.where(row8 < s, pltpu.roll(prev8, s, 0), rolled[:8])
        return jnp.concatenate([head, rolled[8:]], axis=0)
    tail = jnp.where(row8 >= 8 + s, pltpu.roll(next8, s % 8, 0), rolled[n - 8:])
    return jnp.concatenate([rolled[:n - 8], tail], axis=0)


def _conv_silu_body(x_ref, xp_ref, xn_ref, w_ref, b_ref, o_ref, *, halo):
    i = pl.program_id(1)
    tl = x_ref.shape[0]
    prev8 = jnp.where(i == 0, 0.0, xp_ref[halo - 8:, :].astype(F32))
    next8 = jnp.where(i == pl.num_programs(1) - 1, 0.0, xn_ref[:8, :].astype(F32))
    x = x_ref[...].astype(F32)
    acc = jnp.broadcast_to(b_ref[...], (tl, b_ref.shape[1]))
    pad = SSM_CONV // 2
    for k in range(SSM_CONV):
        acc = acc + w_ref[k:k + 1, :] * _shift_rows(x, prev8, next8, pad - k)
    o_ref[...] = (acc / (1.0 + jnp.exp(-acc))).astype(o_ref.dtype)


def conv_silu(proj, col0, w, b, tl=512, tc=512):
    B, L, _ = proj.shape
    C = w.shape[1]
    halo = 16
    cb0 = col0 // tc
    hb = tl // halo
    nhb = L // halo
    return pl.pallas_call(
        functools.partial(_conv_silu_body, halo=halo),
        grid=(B, L // tl, C // tc),
        in_specs=[
            pl.BlockSpec((None, tl, tc), lambda b, i, j: (b, i, cb0 + j)),
            pl.BlockSpec((None, halo, tc), lambda b, i, j: (b, jnp.maximum(i * hb - 1, 0), cb0 + j)),
            pl.BlockSpec((None, halo, tc), lambda b, i, j: (b, jnp.minimum((i + 1) * hb, nhb - 1), cb0 + j)),
            pl.BlockSpec((SSM_CONV, tc), lambda b, i, j: (0, j)),
            pl.BlockSpec((1, tc), lambda b, i, j: (0, j)),
        ],
        out_specs=pl.BlockSpec((None, tl, tc), lambda b, i, j: (b, i, j)),
        out_shape=jax.ShapeDtypeStruct((B, L, C), BF16),
        compiler_params=_cparams("parallel", "parallel", "arbitrary"),
        name="ssm_conv_silu",
    )(proj, proj, proj, w, b.reshape(1, C))


def _split3(a):
    hi = a.astype(BF16)
    r1 = a - hi.astype(F32)
    mid = r1.astype(BF16)
    lo = (r1 - mid.astype(F32)).astype(BF16)
    return jnp.concatenate([hi, mid, lo], axis=1)


def _ssd_chunk_terms(dt_ref, bias_ref, alog_ref):
    Q = SSM_CHUNK
    lane = lax.broadcasted_iota(jnp.int32, (Q, DT_LANES), 1)
    row = lax.broadcasted_iota(jnp.int32, (Q, DT_LANES), 0)
    v = dt_ref[...].astype(F32) + bias_ref[...]
    dt = jnp.maximum(v, 0.0) + jnp.log1p(jnp.exp(-jnp.abs(v)))
    a_row = -jnp.exp(alog_ref[...])
    da = jnp.where(lane < 2 * SSM_HEADS, dt * a_row, 0.0)
    pre, suf = da, da
    s = 1
    while s < Q:
        pre = pre + jnp.where(row >= s, pltpu.roll(pre, s, 0), 0.0)
        suf = suf + jnp.where(row < Q - s, pltpu.roll(suf, Q - s, 0), 0.0)
        s *= 2
    is_fwd = lane < SSM_HEADS
    cs = jnp.where(is_fwd, pre, suf)
    tot = jnp.where(is_fwd[:1], cs[Q - 1:Q, :], cs[0:1, :])
    e = jnp.exp(cs)
    w = jnp.exp(tot - cs) * dt
    return dt, cs, e, w


def _ssd_fwd_body(x_ref, bc_ref, dt_ref, bias_ref, alog_ref, d_ref, exp_ref, y_ref, h_ref):
    Q, G, R, P, N = SSM_CHUNK, SSM_GROUPS, SSM_GROUP_HEADS, SSM_HEAD_DIM, SSM_STATE
    GP = R * P

    @pl.when(pl.program_id(1) == 0)
    def _():
        h_ref[...] = jnp.zeros_like(h_ref)

    dt, cs, e, w = _ssd_chunk_terms(dt_ref, bias_ref, alog_ref)
    cs_t = cs.T
    dt_t = dt.T
    e_x = jnp.dot(_split3(e), exp_ref[:, :SSM_WIDTH], preferred_element_type=F32)
    w_x = jnp.dot(_split3(w), exp_ref[:, :SSM_WIDTH], preferred_element_type=F32)
    x = x_ref[...]
    xf = x.astype(F32)
    xw = (xf * w_x).astype(BF16)
    tri = lax.broadcasted_iota(jnp.int32, (Q, Q), 0) >= lax.broadcasted_iota(jnp.int32, (Q, Q), 1)
    nt = (((1,), (1,)), ((), ()))
    cbs, y_offs = [], []
    for g in range(G):
        bg = bc_ref[:, g * N:(g + 1) * N]
        cg = bc_ref[:, G * N + g * N:G * N + (g + 1) * N]
        cbs.append(lax.dot_general(cg, bg, nt, preferred_element_type=F32))
        gs = slice(g * GP, (g + 1) * GP)
        hg = h_ref[g]
        y_offs.append(jnp.dot(cg, hg.astype(BF16), preferred_element_type=F32))
        bg_t = bg.astype(F32).T.astype(BF16)
        h_ref[g] = hg * e_x[Q - 1:Q, gs] + jnp.dot(bg_t, xw[:, gs], preferred_element_type=F32)
    ys = []
    for h in range(SSM_HEADS):
        hb = SSM_HEADS + h
        arg = jnp.where(tri, jnp.broadcast_to(cs[:, h:h + 1], (Q, Q)) - cs_t[h:h + 1, :],
                        jnp.broadcast_to(cs[:, hb:hb + 1], (Q, Q)) - cs_t[hb:hb + 1, :])
        dtrow = jnp.where(tri, dt_t[h:h + 1, :], dt_t[hb:hb + 1, :])
        wm = (cbs[h // R] * jnp.exp(arg) * dtrow).astype(BF16)
        ys.append(jnp.dot(wm, x[:, h * P:(h + 1) * P], preferred_element_type=F32))
    y_ref[...] = jnp.concatenate(ys, axis=1) + jnp.concatenate(y_offs, axis=1) * e_x + d_ref[...] * xf


def _ssd_bwd_body(x_ref, bc_ref, dt_ref, bias_ref, alog_ref, exp_ref, yf_ref, z_ref, gain_ref, o_ref, h_ref):
    Q, G, R, P, N = SSM_CHUNK, SSM_GROUPS, SSM_GROUP_HEADS, SSM_HEAD_DIM, SSM_STATE
    GP = R * P

    @pl.when(pl.program_id(1) == 0)
    def _():
        h_ref[...] = jnp.zeros_like(h_ref)

    _, _, e, w = _ssd_chunk_terms(dt_ref, bias_ref, alog_ref)
    e_x = jnp.dot(_split3(e), exp_ref[:, SSM_WIDTH:], preferred_element_type=F32)
    w_x = jnp.dot(_split3(w), exp_ref[:, SSM_WIDTH:], preferred_element_type=F32)
    xw = (x_ref[...].astype(F32) * w_x).astype(BF16)
    ys = []
    for g in range(G):
        bg = bc_ref[:, g * N:(g + 1) * N]
        cg = bc_ref[:, G * N + g * N:G * N + (g + 1) * N]
        gs = slice(g * GP, (g + 1) * GP)
        hg = h_ref[g]
        ys.append(jnp.dot(cg, hg.astype(BF16), preferred_element_type=F32))
        bg_t = bg.astype(F32).T.astype(BF16)
        h_ref[g] = hg * e_x[0:1, gs] + jnp.dot(bg_t, xw[:, gs], preferred_element_type=F32)
    y = yf_ref[...] + jnp.concatenate(ys, axis=1) * e_x
    z = z_ref[...].astype(F32)
    y = y * (z / (1.0 + jnp.exp(-z)))
    o_ref[...] = _rms(y, gain_ref[...]).astype(o_ref.dtype)


def _ssd_expand_matrix():
    ex = np.zeros((DT_LANES, 2 * SSM_WIDTH), np.float32)
    for h in range(2 * SSM_HEADS):
        ex[h, h * SSM_HEAD_DIM:(h + 1) * SSM_HEAD_DIM] = 1.0
    return jnp.asarray(np.tile(ex, (3, 1)), BF16)


def ssd_mixer(proj, xs, bc, dt_bias, a_log, d_row, gain):
    B, L, _ = proj.shape
    Q = SSM_CHUNK
    nc = L // Q
    ex = _ssd_expand_matrix()
    full = lambda shape: pl.BlockSpec(shape, lambda b, c: (0,) * len(shape))
    state = pltpu.VMEM((SSM_GROUPS, SSM_STATE, SSM_GROUP_HEADS * SSM_HEAD_DIM), F32)

    def chunk(width, col_block, rev):
        if rev:
            return pl.BlockSpec((None, Q, width), lambda b, c: (b, nc - 1 - c, col_block))
        return pl.BlockSpec((None, Q, width), lambda b, c: (b, c, col_block))

    y_f = pl.pallas_call(
        _ssd_fwd_body,
        grid=(B, nc),
        in_specs=[chunk(SSM_WIDTH, 0, False), chunk(2 * SSM_GROUPS * SSM_STATE, 0, False),
                  chunk(DT_LANES, COL_DT // DT_LANES, False),
                  full((1, DT_LANES)), full((1, DT_LANES)), full((1, SSM_WIDTH)), full(ex.shape)],
        out_specs=chunk(SSM_WIDTH, 0, False),
        out_shape=jax.ShapeDtypeStruct((B, L, SSM_WIDTH), F32),
        scratch_shapes=[state],
        compiler_params=_cparams("parallel", "arbitrary"),
        name="ssd_fwd",
    )(xs, bc, proj, dt_bias, a_log, d_row, ex)
    o_b = pl.pallas_call(
        _ssd_bwd_body,
        grid=(B, nc),
        in_specs=[chunk(SSM_WIDTH, 0, True), chunk(2 * SSM_GROUPS * SSM_STATE, 0, True),
                  chunk(DT_LANES, COL_DT // DT_LANES, True),
                  full((1, DT_LANES)), full((1, DT_LANES)), full(ex.shape),
                  chunk(SSM_WIDTH, 0, True), chunk(SSM_WIDTH, COL_Z // SSM_WIDTH, True), full((1, SSM_WIDTH))],
        out_specs=chunk(SSM_WIDTH, 0, True),
        out_shape=jax.ShapeDtypeStruct((B, L, SSM_WIDTH), BF16),
        scratch_shapes=[state],
        compiler_params=_cparams("parallel", "arbitrary"),
        name="ssd_bwd",
    )(xs, bc, proj, dt_bias, a_log, ex, y_f, proj, gain)
    return o_b.reshape(B * L, SSM_WIDTH)


def na_bias_table(rpb, rows):
    nrb = rows // 2
    ii = np.array([0, 1, 2, nrb - 2, nrb - 1])
    qrow = ii[:, None] * 2 + np.arange(2)[None, :]
    kstart = np.clip(ii * 2 - NA_ROWS // 2, 0, rows - NA_KEY_ROWS)
    krow = kstart[:, None] + np.arange(NA_KEY_ROWS)[None, :]
    wr = np.clip(qrow - NA_ROWS // 2, 0, rows - NA_ROWS)
    row_ok = (krow[:, None, :] >= wr[:, :, None]) & (krow[:, None, :] < wr[:, :, None] + NA_ROWS)
    row_off = np.clip(krow[:, None, :] - qrow[:, :, None] + NA_ROWS - 1, 0, 2 * NA_ROWS - 2)
    qcol = np.arange(GRID_W)
    wc = np.clip(qcol - NA_COLS // 2, 0, GRID_W - NA_COLS)
    col_ok = (qcol[None, :] >= wc[:, None]) & (qcol[None, :] < wc[:, None] + NA_COLS)
    col_off = np.clip(qcol[None, :] - qcol[:, None] + NA_COLS - 1, 0, 2 * NA_COLS - 2)
    row_sel = (row_off[..., None] == np.arange(2 * NA_ROWS - 1)).astype(np.float32)
    col_sel = (col_off[..., None] == np.arange(2 * NA_COLS - 1)).astype(np.float32)
    hi = lax.Precision.HIGHEST
    t1 = jnp.einsum('pqkr,hrc->phqkc', row_sel, rpb.astype(F32), precision=hi)
    bias = jnp.einsum('phqkc,xyc->phqxky', t1, col_sel, precision=hi)
    mask = row_ok[:, None, :, None, :, None] & col_ok[None, None, None, :, None, :]
    tab = jnp.where(mask, bias, NEG_BIG)
    return tab.reshape(5, rpb.shape[0], 2 * GRID_W, NA_KEY_ROWS * GRID_W)


def _na_body(q_ref, k0, k1, k2, k3, k4, v0, v1, v2, v3, v4, bias_ref, o_ref):
    scale = HEAD_DIM ** -0.5
    nt = (((1,), (1,)), ((), ()))
    k_refs = (k0, k1, k2, k3, k4)
    v_refs = (v0, v1, v2, v3, v4)
    tq = q_ref.shape[0]

    def scores(h):
        hs = slice(h * HEAD_DIM, (h + 1) * HEAD_DIM)
        q = q_ref[:, hs]
        s = jnp.concatenate([lax.dot_general(q, kr[:, hs], nt, preferred_element_type=F32) for kr in k_refs], axis=1)
        return s * scale + bias_ref[h]

    s_next = scores(0)
    for h in range(NA_HEADS):
        hs = slice(h * HEAD_DIM, (h + 1) * HEAD_DIM)
        s = s_next
        if h + 1 < NA_HEADS:
            s_next = scores(h + 1)
        mx = jnp.max(s, axis=-1, keepdims=True)
        p = jnp.exp(s - mx)
        l = jnp.sum(p, axis=-1, keepdims=True)
        pb = p.astype(BF16)
        o = jnp.dot(pb[:, :tq], v_refs[0][:, hs], preferred_element_type=F32)
        for j in range(1, 5):
            o += jnp.dot(pb[:, j * tq:(j + 1) * tq], v_refs[j][:, hs], preferred_element_type=F32)
        o_ref[:, hs] = (o / l).astype(o_ref.dtype)


def neighbourhood_attention(proj, bias_tab):
    B, L, _ = proj.shape
    tq = 2 * GRID_W
    nrb = L // tq
    W = NA_WIDTH

    def pattern(i):
        return jnp.where(i < 2, i, jnp.where(i >= nrb - 2, i - (nrb - 5), 2))

    def kv(j, col):
        return pl.BlockSpec((None, tq, W), lambda b, i: (b, jnp.clip(i - 2, 0, nrb - 5) + j, col))

    cq, ck, cv = COL_CQ // W, COL_CK // W, COL_CV // W
    o = pl.pallas_call(
        _na_body,
        grid=(B, nrb),
        in_specs=[pl.BlockSpec((None, tq, W), lambda b, i: (b, i, cq))]
        + [kv(j, ck) for j in range(5)] + [kv(j, cv) for j in range(5)]
        + [pl.BlockSpec((None, NA_HEADS, tq, 5 * tq), lambda b, i: (pattern(i), 0, 0, 0))],
        out_specs=pl.BlockSpec((None, tq, W), lambda b, i: (b, i, 0)),
        out_shape=jax.ShapeDtypeStruct((B, L, W), BF16),
        compiler_params=_cparams("parallel", "arbitrary"),
        name="neighbourhood_attn",
    )(*([proj] * 11), bias_tab)
    return o.reshape(B * L, W)


def _mem_attn_body(xn_ref, x_ref, kv_ref, wq_ref, wo_ref, gp_ref, gn_ref, xo_ref, xnn_ref, o_ref):
    scale = HEAD_DIM ** -0.5
    nt = (((1,), (1,)), ((), ()))
    q = jnp.dot(xn_ref[...], wq_ref[...], preferred_element_type=F32).astype(BF16)
    outs = []

    def scores(h):
        hs = slice(h * HEAD_DIM, (h + 1) * HEAD_DIM)
        return lax.dot_general(q[:, hs], kv_ref[:, hs], nt, preferred_element_type=F32) * scale

    s_next = scores(0)
    for h in range(MEM_HEADS):
        vs = slice(MEM_WIDTH + h * HEAD_DIM, MEM_WIDTH + (h + 1) * HEAD_DIM)
        s = s_next
        if h + 1 < MEM_HEADS:
            s_next = scores(h + 1)
        mx = jnp.max(s, axis=-1, keepdims=True)
        p = jnp.exp(s - mx)
        l = jnp.sum(p, axis=-1, keepdims=True)
        o = jnp.dot(p.astype(BF16), kv_ref[:, vs], preferred_element_type=F32) / l
        outs.append(o.astype(BF16))
    o_ref[...] = jnp.concatenate(outs, axis=1)
    _accumulate_and_finish(xo_ref, o_ref, wo_ref, x_ref, gp_ref, gn_ref, xnn_ref, first=True)


def memory_attention(xn, x, kv, wq, wo, g_post, g_next, tm=256):
    B, L, D = x.shape
    tok = pl.BlockSpec((None, tm, D), lambda b, i: (b, i, 0))
    full = lambda shape: pl.BlockSpec(shape, lambda b, i: (0,) * len(shape))
    return pl.pallas_call(
        _mem_attn_body,
        grid=(B, L // tm),
        in_specs=[tok, tok, pl.BlockSpec((None, MEM_TOKENS, 2 * MEM_WIDTH), lambda b, i: (b, 0, 0)),
                  full((D, MEM_WIDTH)), full((MEM_WIDTH, D)), full((1, D)), full((1, D))],
        out_specs=[tok, tok],
        out_shape=[jax.ShapeDtypeStruct((B, L, D), F32), jax.ShapeDtypeStruct((B, L, D), BF16)],
        scratch_shapes=[pltpu.VMEM((tm, MEM_WIDTH), BF16)],
        compiler_params=_cparams("parallel", "arbitrary"),
        name="memory_attn",
    )(xn, x, kv, wq, wo, g_post.reshape(1, D), g_next.reshape(1, D))


def _ffn_up_body(xn_ref, xp_ref, xnx_ref, wg_ref, wv_ref, cw_ref, o_ref, xe_ref, ext_ref, *, halo, tiles_per_seq):
    i = pl.program_id(0)
    tm = xn_ref.shape[0]

    @pl.when(pl.program_id(1) == 0)
    def _():
        xe_ref[0:halo, :] = xp_ref[...]
        xe_ref[halo:halo + tm, :] = xn_ref[...]
        xe_ref[halo + tm:, :] = xnx_ref[...]

    pos = lax.rem(i, tiles_per_seq)
    gate = jnp.dot(xe_ref[...], wg_ref[...], preferred_element_type=F32)
    ext_ref[0:halo, :] = jnp.where(pos == 0, 0.0, gate[0:halo])
    ext_ref[halo:halo + tm, :] = gate[halo:halo + tm]
    ext_ref[halo + tm:, :] = jnp.where(pos == tiles_per_seq - 1, 0.0, gate[halo + tm:])
    val = jnp.dot(xn_ref[...], wv_ref[...], preferred_element_type=F32)
    conv = (cw_ref[0:1, :] * ext_ref[halo - 1:halo - 1 + tm, :]
            + cw_ref[1:2, :] * ext_ref[halo:halo + tm, :]
            + cw_ref[2:3, :] * ext_ref[halo + 1:halo + 1 + tm, :])
    cdf = 0.5 * (1.0 + jnp.tanh(math.sqrt(2.0 / math.pi) * (conv + 0.044715 * (conv * conv * conv))))
    o_ref[...] = (conv * cdf * val).astype(o_ref.dtype)


def ffn_up_gated(xn, w_up, conv_w, seq_len, tm=1024, tn=512):
    T, D = xn.shape
    F = w_up.shape[1] // 2
    halo = 16
    nj = F // tn
    hb = tm // halo
    nhb = T // halo
    return pl.pallas_call(
        functools.partial(_ffn_up_body, halo=halo, tiles_per_seq=seq_len // tm),
        grid=(T // tm, nj),
        in_specs=[
            pl.BlockSpec((tm, D), lambda i, j: (i, 0)),
            pl.BlockSpec((halo, D), lambda i, j: (jnp.maximum(i * hb - 1, 0), 0)),
            pl.BlockSpec((halo, D), lambda i, j: (jnp.minimum((i + 1) * hb, nhb - 1), 0)),
            pl.BlockSpec((D, tn), lambda i, j: (0, j)),
            pl.BlockSpec((D, tn), lambda i, j: (0, nj + j)),
            pl.BlockSpec((3, tn), lambda i, j: (0, j)),
        ],
        out_specs=pl.BlockSpec((tm, tn), lambda i, j: (i, j)),
        out_shape=jax.ShapeDtypeStruct((T, F), BF16),
        scratch_shapes=[pltpu.VMEM((tm + 2 * halo, D), BF16), pltpu.VMEM((tm + 2 * halo, tn), F32)],
        compiler_params=_cparams("parallel", "arbitrary"),
        name="ffn_up",
    )(xn, xn, xn, w_up, w_up, conv_w)


def _mm_res_body(h_ref, w_ref, x_ref, gp_ref, gn_ref, xo_ref, xn_ref):
    @pl.when(pl.program_id(1) == 0)
    def _():
        _accumulate(xo_ref, h_ref[...], w_ref, first=True)

    last = pl.num_programs(1) - 1

    @pl.when((pl.program_id(1) > 0) & (pl.program_id(1) < last))
    def _():
        _accumulate(xo_ref, h_ref[...], w_ref)

    @pl.when(pl.program_id(1) == last)
    def _():
        _accumulate_and_finish(xo_ref, h_ref, w_ref, x_ref, gp_ref, gn_ref, xn_ref)


def matmul_residual(h, w, x, g_post, g_next, name, tm=512, tk=512):
    T, K = h.shape
    D = w.shape[1]
    row = lambda i, k: (i, 0)
    return pl.pallas_call(
        _mm_res_body,
        grid=(T // tm, K // tk),
        in_specs=[
            pl.BlockSpec((tm, tk), lambda i, k: (i, k)),
            pl.BlockSpec((tk, D), lambda i, k: (k, 0)),
            pl.BlockSpec((tm, D), row),
            pl.BlockSpec((1, D), lambda i, k: (0, 0)),
            pl.BlockSpec((1, D), lambda i, k: (0, 0)),
        ],
        out_specs=[pl.BlockSpec((tm, D), row), pl.BlockSpec((tm, D), row)],
        out_shape=[jax.ShapeDtypeStruct((T, D), F32), jax.ShapeDtypeStruct((T, D), BF16)],
        compiler_params=_cparams("parallel", "arbitrary"),
        name=name,
    )(h, w, x, g_post.reshape(1, D), g_next.reshape(1, D))


def _cast_pad_body(src_ref, w_ref, o_ref, *, pad_blocks):
    del src_ref
    j = pl.program_id(0)
    is_pad = functools.reduce(lambda a, b: a | b, [j == pb for pb in pad_blocks])

    @pl.when(is_pad)
    def _():
        o_ref[...] = jnp.zeros_like(o_ref)

    @pl.when(jnp.logical_not(is_pad))
    def _():
        o_ref[...] = w_ref[...].astype(o_ref.dtype)


def cast_pad_blocks(w_stack, layer, axis, blk, segments):
    _, R, C = w_stack.shape
    src_blocks, pad_blocks = [], []
    for start, length in segments:
        n = length // blk
        src_blocks += [start // blk + b for b in range(n)]
        pad_blocks.append(len(src_blocks))
        src_blocks.append(start // blk + n - 1)
    src = jnp.asarray(np.array(src_blocks, np.int32))
    nb = len(src_blocks)
    if axis == 1:
        block, out_shape = (None, R, blk), (R, nb * blk)
        in_map = lambda j, s: (layer, 0, s[j])
        out_spec = pl.BlockSpec((R, blk), lambda j, s: (0, j))
    else:
        block, out_shape = (None, blk, C), (nb * blk, C)
        in_map = lambda j, s: (layer, s[j], 0)
        out_spec = pl.BlockSpec((blk, C), lambda j, s: (j, 0))
    return pl.pallas_call(
        functools.partial(_cast_pad_body, pad_blocks=tuple(pad_blocks)),
        grid_spec=pltpu.PrefetchScalarGridSpec(
            num_scalar_prefetch=1, grid=(nb,),
            in_specs=[pl.BlockSpec(block, in_map)], out_specs=out_spec),
        out_shape=jax.ShapeDtypeStruct(out_shape, BF16),
        compiler_params=_cparams("arbitrary"),
        name="weight_cast_pad",
    )(src, w_stack)


def prepare_layer_params(p, l):
    w_in = p["w_in"][l]
    zeros = lambda r, c: jnp.zeros((r, c), w_in.dtype)
    a_end = 3 * A_WIDTH + SSM_WIDTH + SSM_WIDTH
    bc0 = a_end
    dt0 = bc0 + 2 * SSM_GROUPS * SSM_STATE
    c0 = dt0 + 2 * SSM_HEADS
    a_cols = [w_in[:, s * A_WIDTH + g * A_OUT:s * A_WIDTH + (g + 1) * A_OUT]
              for g in range(len(DILATED_GROUPS)) for s in range(3)]
    w_in_p = jnp.concatenate(
        a_cols + [w_in[:, 3 * A_WIDTH:a_end], w_in[:, dt0:c0], zeros(D_MODEL, COL_CQ - COL_DT - 2 * SSM_HEADS),
                  w_in[:, c0:], w_in[:, bc0:dt0]], axis=1).astype(BF16)
    fpad = D_FF_PAD - D_FF
    w_up_p = cast_pad_blocks(p["ffn_w_up"], l, 1, fpad, [(0, D_FF), (D_FF, D_FF)])
    w_down_p = cast_pad_blocks(p["ffn_w_down"], l, 0, fpad, [(0, D_FF)])
    lanes = lambda v: jnp.concatenate([v.reshape(-1), jnp.zeros((DT_LANES - 2 * SSM_HEADS,), F32)]).reshape(1, DT_LANES)
    conv_w = p["ssm_conv_w"][l]
    conv_b = p["ssm_conv_b"][l]
    return dict(
        w_in=w_in_p, w_up=w_up_p, w_down=w_down_p,
        w_out=p["w_out"][l].astype(BF16),
        mem_wq=p["mem_wq"][l].astype(BF16), mem_wo=p["mem_wo"][l].astype(BF16),
        mem_wkv=jnp.concatenate([p["mem_wk"][l], p["mem_wv"][l]], axis=1).astype(BF16),
        ffn_conv_w=jnp.concatenate([p["ffn_conv_w"][l], jnp.zeros((3, fpad), F32)], axis=1),
        conv_w_x=conv_w[:, :SSM_WIDTH], conv_b_x=conv_b[:SSM_WIDTH],
        conv_w_bc=conv_w[:, SSM_WIDTH:], conv_b_bc=conv_b[SSM_WIDTH:],
        dt_bias=lanes(p["ssm_dt_bias"][l]), a_log=lanes(p["ssm_a_log"][l]),
        d_row=jnp.repeat(p["ssm_d"][l], SSM_HEAD_DIM).reshape(1, SSM_WIDTH),
        ssm_norm=p["ssm_norm"][l].reshape(1, SSM_WIDTH),
        na_rpb=p["na_rpb"][l],
        mix_norm_pre=p["mix_norm_pre"][l], mix_norm_post=p["mix_norm_post"][l],
        mem_norm_pre=p["mem_norm_pre"][l], mem_norm_post=p["mem_norm_post"][l], mem_norm_kv=p["mem_norm_kv"][l],
        ffn_norm_pre=p["ffn_norm_pre"][l], ffn_norm_post=p["ffn_norm_post"][l],
    )


def hybrid_mixer(xn, lp, tab, B, L):
    proj = matmul(xn, lp["w_in"], 1024, 1024, "in_proj").reshape(B, L, IN_COLS_PAD)
    outs, lses = [], []
    for g, (window, dil) in enumerate(DILATED_GROUPS):
        o, lse = dilated_attention(proj, tab, g, dil, window // (2 * dil))
        outs.append(o)
        lses.append(lse)
    o_a = combine_groups(outs, lses)
    xs = conv_silu(proj, COL_X, lp["conv_w_x"], lp["conv_b_x"])
    bc = conv_silu(proj, COL_B, lp["conv_w_bc"], lp["conv_b_bc"])
    o_b = ssd_mixer(proj, xs, bc, lp["dt_bias"], lp["a_log"], lp["d_row"], lp["ssm_norm"])
    o_c = neighbourhood_attention(proj, na_bias_table(lp["na_rpb"], L // GRID_W))
    return o_a, o_b, o_c


def encoder_layer(x, xn, mem, lp, g_next, tab):
    B, L, D = x.shape
    T = B * L
    o_a, o_b, o_c = hybrid_mixer(xn, lp, tab, B, L)
    x2, xn = mixer_out_proj(o_a, o_b, o_c, lp["w_out"], x.reshape(T, D), lp["mix_norm_post"], lp["mem_norm_pre"])
    mem_n = rmsnorm(mem.reshape(B * MEM_TOKENS, D), lp["mem_norm_kv"])
    kv = matmul(mem_n, lp["mem_wkv"], 256, 1024, "mem_kv_proj").reshape(B, MEM_TOKENS, 2 * MEM_WIDTH)
    x3, xn = memory_attention(xn.reshape(B, L, D), x2.reshape(B, L, D), kv, lp["mem_wq"], lp["mem_wo"],
                              lp["mem_norm_post"], lp["ffn_norm_pre"])
    act = ffn_up_gated(xn.reshape(T, D), lp["w_up"], lp["ffn_conv_w"], L)
    x4, xn = matmul_residual(act, lp["w_down"], x3.reshape(T, D), lp["ffn_norm_post"], g_next, "ffn_down")
    return x4.reshape(B, L, D), xn


def kernel(x_prompt, x_sample, mem_prompt, mem_sample, mix_norm_pre, mix_norm_post, w_in, ssm_conv_w, ssm_conv_b,
           ssm_a_log, ssm_dt_bias, ssm_d, ssm_norm, na_rpb, w_out, mem_norm_pre, mem_norm_post, mem_norm_kv,
           mem_wq, mem_wk, mem_wv, mem_wo, ffn_norm_pre, ffn_norm_post, ffn_w_up, ffn_conv_w, ffn_w_down):
    p = dict(mix_norm_pre=mix_norm_pre, mix_norm_post=mix_norm_post, w_in=w_in, ssm_conv_w=ssm_conv_w,
             ssm_conv_b=ssm_conv_b, ssm_a_log=ssm_a_log, ssm_dt_bias=ssm_dt_bias, ssm_d=ssm_d, ssm_norm=ssm_norm,
             na_rpb=na_rpb, w_out=w_out, mem_norm_pre=mem_norm_pre, mem_norm_post=mem_norm_post,
             mem_norm_kv=mem_norm_kv, mem_wq=mem_wq, mem_wk=mem_wk, mem_wv=mem_wv, mem_wo=mem_wo,
             ffn_norm_pre=ffn_norm_pre, ffn_norm_post=ffn_norm_post, ffn_w_up=ffn_w_up, ffn_conv_w=ffn_conv_w,
             ffn_w_down=ffn_w_down)
    groups = [(x_prompt, mem_prompt), (x_sample, mem_sample)]
    tabs = [rotary_table(x.shape[1]) for x, _ in groups]
    xs = [x for x, _ in groups]
    xns = [rmsnorm(x.reshape(-1, D_MODEL), mix_norm_pre[0]) for x in xs]
    for l in range(DEPTH):
        lp = prepare_layer_params(p, l)
        g_next = mix_norm_pre[l + 1] if l + 1 < DEPTH else jnp.ones((D_MODEL,), F32)
        for gi, (_, mem) in enumerate(groups):
            xs[gi], xns[gi] = encoder_layer(xs[gi], xns[gi], mem, lp, g_next, tabs[gi])
    return (xs[0], xs[1])
```

```python
import functools
import math

import jax
import jax.numpy as jnp
import numpy as np
from jax import lax
from jax.experimental import pallas as pl
from jax.experimental.pallas import tpu as pltpu

F32 = jnp.float32
BF16 = jnp.bfloat16

D_MODEL = 4096
DEPTH = 4
HEAD_DIM = 128
A_WIDTH = 1536
SSM_WIDTH = 1536
NA_WIDTH = 1024
DILATED_GROUPS = ((128, 1), (512, 4), (2048, 16))
A_GROUP_HEADS = 4
A_OUT = A_GROUP_HEADS * HEAD_DIM
ROT_DIM = 32
ROPE_THETA = 500000.0
SSM_HEAD_DIM = 64
SSM_HEADS = 24
SSM_GROUPS = 8
SSM_GROUP_HEADS = SSM_HEADS // SSM_GROUPS
SSM_STATE = 128
SSM_CONV = 5
SSM_CHUNK = 128
NA_HEADS = 8
GRID_W = 64
NA_ROWS = 8
NA_COLS = 16
NA_KEY_ROWS = 10
MEM_TOKENS = 256
MEM_HEADS = 4
MEM_WIDTH = MEM_HEADS * HEAD_DIM
D_FF = 11008
D_FF_PAD = 11264
EPS = 1e-6

COL_A_GROUP = 3 * A_OUT
COL_Z, COL_X, COL_DT = 4608, 6144, 7680
COL_CQ, COL_CK, COL_CV, COL_B, COL_C = 8192, 9216, 10240, 11264, 12288
IN_COLS_PAD = 13312
DT_LANES = 128

VMEM_LIMIT = 56 * 1024 * 1024
NEG_BIG = -1e30


def _cparams(*sem):
    return pltpu.CompilerParams(dimension_semantics=sem, vmem_limit_bytes=VMEM_LIMIT)


def _rms(x, g):
    ms = jnp.mean(x * x, axis=-1, keepdims=True)
    return x * lax.rsqrt(ms + EPS) * g


def _rmsnorm_body(x_ref, g_ref, o_ref):
    o_ref[...] = _rms(x_ref[...], g_ref[...]).astype(o_ref.dtype)


def rmsnorm(x, g, tm=256):
    T, D = x.shape
    return pl.pallas_call(
        _rmsnorm_body,
        grid=(T // tm,),
        in_specs=[pl.BlockSpec((tm, D), lambda i: (i, 0)), pl.BlockSpec((1, D), lambda i: (0, 0))],
        out_specs=pl.BlockSpec((tm, D), lambda i: (i, 0)),
        out_shape=jax.ShapeDtypeStruct((T, D), BF16),
        compiler_params=_cparams("parallel"),
        name="rmsnorm",
    )(x, g.reshape(1, D))


def _mm_body(x_ref, w_ref, o_ref):
    o_ref[...] = jnp.dot(x_ref[...], w_ref[...], preferred_element_type=F32).astype(o_ref.dtype)


def matmul(x, w, tm, tn, name):
    T, K = x.shape
    N = w.shape[1]
    return pl.pallas_call(
        _mm_body,
        grid=(T // tm, N // tn),
        in_specs=[pl.BlockSpec((tm, K), lambda i, j: (i, 0)), pl.BlockSpec((K, tn), lambda i, j: (0, j))],
        out_specs=pl.BlockSpec((tm, tn), lambda i, j: (i, j)),
        out_shape=jax.ShapeDtypeStruct((T, N), BF16),
        compiler_params=_cparams("parallel", "arbitrary"),
        name=name,
    )(x, w)


ROW_SLAB = 128
COL_SLAB = 1024


def _accumulate(xo_ref, lhs, w_ref, first=False):
    for c in range(0, xo_ref.shape[1], COL_SLAB):
        cols = slice(c, c + COL_SLAB)
        part = jnp.dot(lhs, w_ref[:, cols], preferred_element_type=F32)
        if first:
            xo_ref[:, cols] = part
        else:
            xo_ref[:, cols] += part


def _residual_rows(rows, xo_ref, x_ref, gp_ref, gn_ref, xn_ref):
    x_new = x_ref[rows, :] + _rms(xo_ref[rows, :], gp_ref[...])
    xo_ref[rows, :] = x_new
    xn_ref[rows, :] = _rms(x_new, gn_ref[...]).astype(xn_ref.dtype)


def _accumulate_and_finish(xo_ref, lhs_ref, w_ref, x_ref, gp_ref, gn_ref, xn_ref, first=False):
    slabs = [slice(r, r + ROW_SLAB) for r in range(0, xo_ref.shape[0], ROW_SLAB)]

    def matmul_rows(rows):
        part = jnp.dot(lhs_ref[rows, :], w_ref[...], preferred_element_type=F32)
        if first:
            xo_ref[rows, :] = part
        else:
            xo_ref[rows, :] += part

    matmul_rows(slabs[0])
    for i, rows in enumerate(slabs):
        if i + 1 < len(slabs):
            matmul_rows(slabs[i + 1])
        _residual_rows(rows, xo_ref, x_ref, gp_ref, gn_ref, xn_ref)


def _out_proj_body(oa_ref, ob_ref, oc_ref, w_ref, x_ref, gp_ref, gn_ref, xo_ref, xn_ref):
    ka = oa_ref.shape[1]
    kb = ka + ob_ref.shape[1]
    for c in range(0, xo_ref.shape[1], COL_SLAB):
        cols = slice(c, c + COL_SLAB)
        acc = jnp.dot(oa_ref[...], w_ref[0:ka, cols], preferred_element_type=F32)
        acc += jnp.dot(ob_ref[...], w_ref[ka:kb, cols], preferred_element_type=F32)
        acc += jnp.dot(oc_ref[...], w_ref[kb:, cols], preferred_element_type=F32)
        xo_ref[:, cols] = acc
    for r in range(0, xo_ref.shape[0], ROW_SLAB):
        _residual_rows(slice(r, r + ROW_SLAB), xo_ref, x_ref, gp_ref, gn_ref, xn_ref)


def mixer_out_proj(o_a, o_b, o_c, w, x, g_post, g_next, tm=256):
    T = x.shape[0]
    K, D = w.shape
    row = lambda i: (i, 0)
    const = lambda i: (0, 0)
    return pl.pallas_call(
        _out_proj_body,
        grid=(T // tm,),
        in_specs=[
            pl.BlockSpec((tm, o_a.shape[1]), row),
            pl.BlockSpec((tm, o_b.shape[1]), row),
            pl.BlockSpec((tm, o_c.shape[1]), row),
            pl.BlockSpec((K, D), const, pipeline_mode=pl.Buffered(1)),
            pl.BlockSpec((tm, D), row),
            pl.BlockSpec((1, D), const),
            pl.BlockSpec((1, D), const),
        ],
        out_specs=[pl.BlockSpec((tm, D), row), pl.BlockSpec((tm, D), row)],
        out_shape=[jax.ShapeDtypeStruct((T, D), F32), jax.ShapeDtypeStruct((T, D), BF16)],
        compiler_params=_cparams("arbitrary"),
        name="out_proj",
    )(o_a, o_b, o_c, w, x, g_post.reshape(1, D), g_next.reshape(1, D))


def rotary_table(L):
    half = ROT_DIM // 2
    inv_freq = jnp.exp(-math.log(ROPE_THETA) * jnp.arange(half, dtype=F32) / half)
    ang = jnp.arange(L, dtype=F32)[:, None] * inv_freq[None, :]
    cos, sin = jnp.cos(ang), jnp.sin(ang)
    ones = jnp.ones((L, HEAD_DIM - ROT_DIM), F32)
    zeros = jnp.zeros((L, HEAD_DIM - ROT_DIM), F32)
    return jnp.concatenate([cos, cos, ones, -sin, sin, zeros], axis=1)


def _rope(t, tab):
    lane = lax.broadcasted_iota(jnp.int32, t.shape, 1)
    partner = jnp.where(lane < ROT_DIM // 2, pltpu.roll(t, HEAD_DIM - ROT_DIM // 2, 1), pltpu.roll(t, ROT_DIM // 2, 1))
    return t * tab[:, :HEAD_DIM] + partner * tab[:, HEAD_DIM:]


def _dilated_body(q_ref, kp_ref, kc_ref, kn_ref, vp_ref, vc_ref, vn_ref, tp_ref, tc_ref, tn_ref,
                  o_ref, lse_ref, *, radius, m_len):
    m = pl.program_id(2)
    tq = q_ref.shape[0]
    row = lax.broadcasted_iota(jnp.int32, (tq, 3 * tq), 0)
    col = lax.broadcasted_iota(jnp.int32, (tq, 3 * tq), 1)
    kpos = (m - 1) * tq + col
    rel = col - tq - row
    valid = (jnp.abs(rel) <= radius) & (kpos >= 0) & (kpos < m_len)
    scale = HEAD_DIM ** -0.5
    nt = (((1,), (1,)), ((), ()))

    def scores(h):
        hs = slice(h * HEAD_DIM, (h + 1) * HEAD_DIM)
        q = _rope(q_ref[:, hs].astype(F32), tc_ref[...]).astype(BF16)
        parts = []
        for k_ref, t_ref in ((kp_ref, tp_ref), (kc_ref, tc_ref), (kn_ref, tn_ref)):
            kk = _rope(k_ref[:, hs].astype(F32), t_ref[...]).astype(BF16)
            parts.append(lax.dot_general(q, kk, nt, preferred_element_type=F32))
        return jnp.where(valid, jnp.concatenate(parts, axis=1) * scale, NEG_BIG)

    s_next = scores(0)
    for h in range(A_GROUP_HEADS):
        hs = slice(h * HEAD_DIM, (h + 1) * HEAD_DIM)
        s = s_next
        if h + 1 < A_GROUP_HEADS:
            s_next = scores(h + 1)
        mx = jnp.max(s, axis=-1, keepdims=True)
        p = jnp.exp(s - mx)
        l = jnp.sum(p, axis=-1, keepdims=True)
        pb = p.astype(BF16)
        o = jnp.dot(pb[:, :tq], vp_ref[:, hs], preferred_element_type=F32)
        o += jnp.dot(pb[:, tq:2 * tq], vc_ref[:, hs], preferred_element_type=F32)
        o += jnp.dot(pb[:, 2 * tq:], vn_ref[:, hs], preferred_element_type=F32)
        o_ref[:, hs] = o / l
        lse_ref[:, hs] = jnp.broadcast_to(mx + jnp.log(l), (tq, HEAD_DIM))


def dilated_attention(proj, tab, g, dil, radius):
    B, L, C = proj.shape
    M = L // dil
    tq = 128
    nmb = M // tq
    if dil == 1:
        pv, ncb, cq = proj, C // A_OUT, g * 3
    else:
        pv = proj[:, :, g * COL_A_GROUP:(g + 1) * COL_A_GROUP].reshape(B, M, dil * COL_A_GROUP)
        ncb, cq = 3, 0
    ck, cv = cq + 1, cq + 2
    tv = tab.reshape(M, dil * 2 * HEAD_DIM)

    def blk(off, col0):
        return pl.BlockSpec((None, tq, A_OUT),
                            lambda b, r, m: (b, jnp.clip(m + off, 0, nmb - 1), r * ncb + col0))

    def tblk(off):
        return pl.BlockSpec((tq, 2 * HEAD_DIM), lambda b, r, m: (jnp.clip(m + off, 0, nmb - 1), r))

    out_spec = pl.BlockSpec((None, tq, A_OUT), lambda b, r, m: (b, m, r))
    o, lse = pl.pallas_call(
        functools.partial(_dilated_body, radius=radius, m_len=M),
        grid=(B, dil, nmb),
        in_specs=[blk(0, cq), blk(-1, ck), blk(0, ck), blk(1, ck), blk(-1, cv), blk(0, cv), blk(1, cv),
                  tblk(-1), tblk(0), tblk(1)],
        out_specs=[out_spec, out_spec],
        out_shape=[jax.ShapeDtypeStruct((B, M, dil * A_OUT), F32)] * 2,
        compiler_params=_cparams("parallel", "parallel", "arbitrary"),
        name=f"dilated_attn_d{dil}",
    )(pv, pv, pv, pv, pv, pv, pv, tv, tv, tv)
    return o.reshape(B * L, A_OUT), lse.reshape(B * L, A_OUT)


def _combine_body(o0, o1, o2, l0, l1, l2, out_ref):
    a, b, c = l0[...], l1[...], l2[...]
    mx = jnp.maximum(jnp.maximum(a, b), c)
    ea, eb, ec = jnp.exp(a - mx), jnp.exp(b - mx), jnp.exp(c - mx)
    den = ea + eb + ec
    out_ref[...] = ((ea * o0[...] + eb * o1[...] + ec * o2[...]) / den).astype(out_ref.dtype)


def combine_groups(outs, lses, tm=512):
    T, W = outs[0].shape
    spec = pl.BlockSpec((tm, W), lambda i: (i, 0))
    return pl.pallas_call(
        _combine_body,
        grid=(T // tm,),
        in_specs=[spec] * 6,
        out_specs=spec,
        out_shape=jax.ShapeDtypeStruct((T, W), BF16),
        compiler_params=_cparams("parallel"),
        name="dilated_combine",
    )(*outs, *lses)


def _shift_rows(x, prev8, next8, s):
    if s == 0:
        return x
    n = x.shape[0]
    rolled = pltpu.roll(x, s % n, 0)
    row8 = lax.broadcasted_iota(jnp.int32, prev8.shape, 0)
    if s > 0:
        head = jnp.where(row8 < s, pltpu.roll(prev8, s, 0), rolled[:8])
        return jnp.concatenate([head, rolled[8:]], axis=0)
    tail = jnp.where(row8 >= 8 + s, pltpu.roll(next8, s % 8, 0), rolled[n - 8:])
    return jnp.concatenate([rolled[:n - 8], tail], axis=0)


def _conv_silu_body(x_ref, xp_ref, xn_ref, w_ref, b_ref, o_ref, *, halo):
    i = pl.program_id(1)
    tl = x_ref.shape[0]
    prev8 = jnp.where(i == 0, 0.0, xp_ref[halo - 8:, :].astype(F32))
    next8 = jnp.where(i == pl.num_programs(1) - 1, 0.0, xn_ref[:8, :].astype(F32))
    x = x_ref[...].astype(F32)
    acc = jnp.broadcast_to(b_ref[...], (tl, b_ref.shape[1]))
    pad = SSM_CONV // 2
    for k in range(SSM_CONV):
        acc = acc + w_ref[k:k + 1, :] * _shift_rows(x, prev8, next8, pad - k)
    o_ref[...] = (acc / (1.0 + jnp.exp(-acc))).astype(o_ref.dtype)


def conv_silu(proj, col0, w, b, tl=512, tc=512):
    B, L, _ = proj.shape
    C = w.shape[1]
    halo = 16
    cb0 = col0 // tc
    hb = tl // halo
    nhb = L // halo
    return pl.pallas_call(
        functools.partial(_conv_silu_body, halo=halo),
        grid=(B, L // tl, C // tc),
        in_specs=[
            pl.BlockSpec((None, tl, tc), lambda b, i, j: (b, i, cb0 + j)),
            pl.BlockSpec((None, halo, tc), lambda b, i, j: (b, jnp.maximum(i * hb - 1, 0), cb0 + j)),
            pl.BlockSpec((None, halo, tc), lambda b, i, j: (b, jnp.minimum((i + 1) * hb, nhb - 1), cb0 + j)),
            pl.BlockSpec((SSM_CONV, tc), lambda b, i, j: (0, j)),
            pl.BlockSpec((1, tc), lambda b, i, j: (0, j)),
        ],
        out_specs=pl.BlockSpec((None, tl, tc), lambda b, i, j: (b, i, j)),
        out_shape=jax.ShapeDtypeStruct((B, L, C), BF16),
        compiler_params=_cparams("parallel", "parallel", "arbitrary"),
        name="ssm_conv_silu",
    )(proj, proj, proj, w, b.reshape(1, C))


def _split3(a):
    hi = a.astype(BF16)
    r1 = a - hi.astype(F32)
    mid = r1.astype(BF16)
    lo = (r1 - mid.astype(F32)).astype(BF16)
    return jnp.concatenate([hi, mid, lo], axis=1)


def _ssd_chunk_terms(dt_ref, bias_ref, alog_ref):
    Q = SSM_CHUNK
    lane = lax.broadcasted_iota(jnp.int32, (Q, DT_LANES), 1)
    row = lax.broadcasted_iota(jnp.int32, (Q, DT_LANES), 0)
    v = dt_ref[...].astype(F32) + bias_ref[...]
    dt = jnp.maximum(v, 0.0) + jnp.log1p(jnp.exp(-jnp.abs(v)))
    a_row = -jnp.exp(alog_ref[...])
    da = jnp.where(lane < 2 * SSM_HEADS, dt * a_row, 0.0)
    pre, suf = da, da
    s = 1
    while s < Q:
        pre = pre + jnp.where(row >= s, pltpu.roll(pre, s, 0), 0.0)
        suf = suf + jnp.where(row < Q - s, pltpu.roll(suf, Q - s, 0), 0.0)
        s *= 2
    is_fwd = lane < SSM_HEADS
    cs = jnp.where(is_fwd, pre, suf)
    tot = jnp.where(is_fwd[:1], cs[Q - 1:Q, :], cs[0:1, :])
    e = jnp.exp(cs)
    w = jnp.exp(tot - cs) * dt
    return dt, cs, e, w


def _ssd_fwd_body(x_ref, bc_ref, dt_ref, bias_ref, alog_ref, d_ref, exp_ref, y_ref, h_ref):
    Q, G, R, P, N = SSM_CHUNK, SSM_GROUPS, SSM_GROUP_HEADS, SSM_HEAD_DIM, SSM_STATE
    GP = R * P

    @pl.when(pl.program_id(1) == 0)
    def _():
        h_ref[...] = jnp.zeros_like(h_ref)

    dt, cs, e, w = _ssd_chunk_terms(dt_ref, bias_ref, alog_ref)
    cs_t = cs.T
    dt_t = dt.T
    e_x = jnp.dot(_split3(e), exp_ref[:, :SSM_WIDTH], preferred_element_type=F32)
    w_x = jnp.dot(_split3(w), exp_ref[:, :SSM_WIDTH], preferred_element_type=F32)
    x = x_ref[...]
    xf = x.astype(F32)
    xw = (xf * w_x).astype(BF16)
    tri = lax.broadcasted_iota(jnp.int32, (Q, Q), 0) >= lax.broadcasted_iota(jnp.int32, (Q, Q), 1)
    nt = (((1,), (1,)), ((), ()))
    cbs, y_offs = [], []
    for g in range(G):
        bg = bc_ref[:, g * N:(g + 1) * N]
        cg = bc_ref[:, G * N + g * N:G * N + (g + 1) * N]
        cbs.append(lax.dot_general(cg, bg, nt, preferred_element_type=F32))
        gs = slice(g * GP, (g + 1) * GP)
        hg = h_ref[g]
        y_offs.append(jnp.dot(cg, hg.astype(BF16), preferred_element_type=F32))
        bg_t = bg.astype(F32).T.astype(BF16)
        h_ref[g] = hg * e_x[Q - 1:Q, gs] + jnp.dot(bg_t, xw[:, gs], preferred_element_type=F32)
    ys = []
    for h in range(SSM_HEADS):
        hb = SSM_HEADS + h
        arg = jnp.where(tri, jnp.broadcast_to(cs[:, h:h + 1], (Q, Q)) - cs_t[h:h + 1, :],
                        jnp.broadcast_to(cs[:, hb:hb + 1], (Q, Q)) - cs_t[hb:hb + 1, :])
        dtrow = jnp.where(tri, dt_t[h:h + 1, :], dt_t[hb:hb + 1, :])
        wm = (cbs[h // R] * jnp.exp(arg) * dtrow).astype(BF16)
        ys.append(jnp.dot(wm, x[:, h * P:(h + 1) * P], preferred_element_type=F32))
    y_ref[...] = jnp.concatenate(ys, axis=1) + jnp.concatenate(y_offs, axis=1) * e_x + d_ref[...] * xf


def _ssd_bwd_body(x_ref, bc_ref, dt_ref, bias_ref, alog_ref, exp_ref, yf_ref, z_ref, gain_ref, o_ref, h_ref):
    Q, G, R, P, N = SSM_CHUNK, SSM_GROUPS, SSM_GROUP_HEADS, SSM_HEAD_DIM, SSM_STATE
    GP = R * P

    @pl.when(pl.program_id(1) == 0)
    def _():
        h_ref[...] = jnp.zeros_like(h_ref)

    _, _, e, w = _ssd_chunk_terms(dt_ref, bias_ref, alog_ref)
    e_x = jnp.dot(_split3(e), exp_ref[:, SSM_WIDTH:], preferred_element_type=F32)
    w_x = jnp.dot(_split3(w), exp_ref[:, SSM_WIDTH:], preferred_element_type=F32)
    xw = (x_ref[...].astype(F32) * w_x).astype(BF16)
    ys = []
    for g in range(G):
        bg = bc_ref[:, g * N:(g + 1) * N]
        cg = bc_ref[:, G * N + g * N:G * N + (g + 1) * N]
        gs = slice(g * GP, (g + 1) * GP)
        hg = h_ref[g]
        ys.append(jnp.dot(cg, hg.astype(BF16), preferred_element_type=F32))
        bg_t = bg.astype(F32).T.astype(BF16)
        h_ref[g] = hg * e_x[0:1, gs] + jnp.dot(bg_t, xw[:, gs], preferred_element_type=F32)
    y = yf_ref[...] + jnp.concatenate(ys, axis=1) * e_x
    z = z_ref[...].astype(F32)
    y = y * (z / (1.0 + jnp.exp(-z)))
    o_ref[...] = _rms(y, gain_ref[...]).astype(o_ref.dtype)


def _ssd_expand_matrix():
    ex = np.zeros((DT_LANES, 2 * SSM_WIDTH), np.float32)
    for h in range(2 * SSM_HEADS):
        ex[h, h * SSM_HEAD_DIM:(h + 1) * SSM_HEAD_DIM] = 1.0
    return jnp.asarray(np.tile(ex, (3, 1)), BF16)


def ssd_mixer(proj, xs, bc, dt_bias, a_log, d_row, gain):
    B, L, _ = proj.shape
    Q = SSM_CHUNK
    nc = L // Q
    ex = _ssd_expand_matrix()
    full = lambda shape: pl.BlockSpec(shape, lambda b, c: (0,) * len(shape))
    state = pltpu.VMEM((SSM_GROUPS, SSM_STATE, SSM_GROUP_HEADS * SSM_HEAD_DIM), F32)

    def chunk(width, col_block, rev):
        if rev:
            return pl.BlockSpec((None, Q, width), lambda b, c: (b, nc - 1 - c, col_block))
        return pl.BlockSpec((None, Q, width), lambda b, c: (b, c, col_block))

    y_f = pl.pallas_call(
        _ssd_fwd_body,
        grid=(B, nc),
        in_specs=[chunk(SSM_WIDTH, 0, False), chunk(2 * SSM_GROUPS * SSM_STATE, 0, False),
                  chunk(DT_LANES, COL_DT // DT_LANES, False),
                  full((1, DT_LANES)), full((1, DT_LANES)), full((1, SSM_WIDTH)), full(ex.shape)],
        out_specs=chunk(SSM_WIDTH, 0, False),
        out_shape=jax.ShapeDtypeStruct((B, L, SSM_WIDTH), F32),
        scratch_shapes=[state],
        compiler_params=_cparams("parallel", "arbitrary"),
        name="ssd_fwd",
    )(xs, bc, proj, dt_bias, a_log, d_row, ex)
    o_b = pl.pallas_call(
        _ssd_bwd_body,
        grid=(B, nc),
        in_specs=[chunk(SSM_WIDTH, 0, True), chunk(2 * SSM_GROUPS * SSM_STATE, 0, True),
                  chunk(DT_LANES, COL_DT // DT_LANES, True),
                  full((1, DT_LANES)), full((1, DT_LANES)), full(ex.shape),
                  chunk(SSM_WIDTH, 0, True), chunk(SSM_WIDTH, COL_Z // SSM_WIDTH, True), full((1, SSM_WIDTH))],
        out_specs=chunk(SSM_WIDTH, 0, True),
        out_shape=jax.ShapeDtypeStruct((B, L, SSM_WIDTH), BF16),
        scratch_shapes=[state],
        compiler_params=_cparams("parallel", "arbitrary"),
        name="ssd_bwd",
    )(xs, bc, proj, dt_bias, a_log, ex, y_f, proj, gain)
    return o_b.reshape(B * L, SSM_WIDTH)


def na_bias_table(rpb, rows):
    nrb = rows // 2
    ii = np.array([0, 1, 2, nrb - 2, nrb - 1])
    qrow = ii[:, None] * 2 + np.arange(2)[None, :]
    kstart = np.clip(ii * 2 - NA_ROWS // 2, 0, rows - NA_KEY_ROWS)
    krow = kstart[:, None] + np.arange(NA_KEY_ROWS)[None, :]
    wr = np.clip(qrow - NA_ROWS // 2, 0, rows - NA_ROWS)
    row_ok = (krow[:, None, :] >= wr[:, :, None]) & (krow[:, None, :] < wr[:, :, None] + NA_ROWS)
    row_off = np.clip(krow[:, None, :] - qrow[:, :, None] + NA_ROWS - 1, 0, 2 * NA_ROWS - 2)
    qcol = np.arange(GRID_W)
    wc = np.clip(qcol - NA_COLS // 2, 0, GRID_W - NA_COLS)
    col_ok = (qcol[None, :] >= wc[:, None]) & (qcol[None, :] < wc[:, None] + NA_COLS)
    col_off = np.clip(qcol[None, :] - qcol[:, None] + NA_COLS - 1, 0, 2 * NA_COLS - 2)
    row_sel = (row_off[..., None] == np.arange(2 * NA_ROWS - 1)).astype(np.float32)
    col_sel = (col_off[..., None] == np.arange(2 * NA_COLS - 1)).astype(np.float32)
    hi = lax.Precision.HIGHEST
    t1 = jnp.einsum('pqkr,hrc->phqkc', row_sel, rpb.astype(F32), precision=hi)
    bias = jnp.einsum('phqkc,xyc->phqxky', t1, col_sel, precision=hi)
    mask = row_ok[:, None, :, None, :, None] & col_ok[None, None, None, :, None, :]
    tab = jnp.where(mask, bias, NEG_BIG)
    return tab.reshape(5, rpb.shape[0], 2 * GRID_W, NA_KEY_ROWS * GRID_W)


def _na_body(q_ref, k0, k1, k2, k3, k4, v0, v1, v2, v3, v4, bias_ref, o_ref):
    scale = HEAD_DIM ** -0.5
    nt = (((1,), (1,)), ((), ()))
    k_refs = (k0, k1, k2, k3, k4)
    v_refs = (v0, v1, v2, v3, v4)
    tq = q_ref.shape[0]

    def scores(h):
        hs = slice(h * HEAD_DIM, (h + 1) * HEAD_DIM)
        q = q_ref[:, hs]
        s = jnp.concatenate([lax.dot_general(q, kr[:, hs], nt, preferred_element_type=F32) for kr in k_refs], axis=1)
        return s * scale + bias_ref[h]

    s_next = scores(0)
    for h in range(NA_HEADS):
        hs = slice(h * HEAD_DIM, (h + 1) * HEAD_DIM)
        s = s_next
        if h + 1 < NA_HEADS:
            s_next = scores(h + 1)
        mx = jnp.max(s, axis=-1, keepdims=True)
        p = jnp.exp(s - mx)
        l = jnp.sum(p, axis=-1, keepdims=True)
        pb = p.astype(BF16)
        o = jnp.dot(pb[:, :tq], v_refs[0][:, hs], preferred_element_type=F32)
        for j in range(1, 5):
            o += jnp.dot(pb[:, j * tq:(j + 1) * tq], v_refs[j][:, hs], preferred_element_type=F32)
        o_ref[:, hs] = (o / l).astype(o_ref.dtype)


def neighbourhood_attention(proj, bias_tab):
    B, L, _ = proj.shape
    tq = 2 * GRID_W
    nrb = L // tq
    W = NA_WIDTH

    def pattern(i):
        return jnp.where(i < 2, i, jnp.where(i >= nrb - 2, i - (nrb - 5), 2))

    def kv(j, col):
        return pl.BlockSpec((None, tq, W), lambda b, i: (b, jnp.clip(i - 2, 0, nrb - 5) + j, col))

    cq, ck, cv = COL_CQ // W, COL_CK // W, COL_CV // W
    o = pl.pallas_call(
        _na_body,
        grid=(B, nrb),
        in_specs=[pl.BlockSpec((None, tq, W), lambda b, i: (b, i, cq))]
        + [kv(j, ck) for j in range(5)] + [kv(j, cv) for j in range(5)]
        + [pl.BlockSpec((None, NA_HEADS, tq, 5 * tq), lambda b, i: (pattern(i), 0, 0, 0))],
        out_specs=pl.BlockSpec((None, tq, W), lambda b, i: (b, i, 0)),
        out_shape=jax.ShapeDtypeStruct((B, L, W), BF16),
        compiler_params=_cparams("parallel", "arbitrary"),
        name="neighbourhood_attn",
    )(*([proj] * 11), bias_tab)
    return o.reshape(B * L, W)


def _mem_attn_body(xn_ref, x_ref, kv_ref, wq_ref, wo_ref, gp_ref, gn_ref, xo_ref, xnn_ref, o_ref):
    scale = HEAD_DIM ** -0.5
    nt = (((1,), (1,)), ((), ()))
    q = jnp.dot(xn_ref[...], wq_ref[...], preferred_element_type=F32).astype(BF16)
    outs = []

    def scores(h):
        hs = slice(h * HEAD_DIM, (h + 1) * HEAD_DIM)
        return lax.dot_general(q[:, hs], kv_ref[:, hs], nt, preferred_element_type=F32) * scale

    s_next = scores(0)
    for h in range(MEM_HEADS):
        vs = slice(MEM_WIDTH + h * HEAD_DIM, MEM_WIDTH + (h + 1) * HEAD_DIM)
        s = s_next
        if h + 1 < MEM_HEADS:
            s_next = scores(h + 1)
        mx = jnp.max(s, axis=-1, keepdims=True)
        p = jnp.exp(s - mx)
        l = jnp.sum(p, axis=-1, keepdims=True)
        o = jnp.dot(p.astype(BF16), kv_ref[:, vs], preferred_element_type=F32) / l
        outs.append(o.astype(BF16))
    o_ref[...] = jnp.concatenate(outs, axis=1)
    _accumulate_and_finish(xo_ref, o_ref, wo_ref, x_ref, gp_ref, gn_ref, xnn_ref, first=True)


def memory_attention(xn, x, kv, wq, wo, g_post, g_next, tm=256):
    B, L, D = x.shape
    tok = pl.BlockSpec((None, tm, D), lambda b, i: (b, i, 0))
    full = lambda shape: pl.BlockSpec(shape, lambda b, i: (0,) * len(shape))
    return pl.pallas_call(
        _mem_attn_body,
        grid=(B, L // tm),
        in_specs=[tok, tok, pl.BlockSpec((None, MEM_TOKENS, 2 * MEM_WIDTH), lambda b, i: (b, 0, 0)),
                  full((D, MEM_WIDTH)), full((MEM_WIDTH, D)), full((1, D)), full((1, D))],
        out_specs=[tok, tok],
        out_shape=[jax.ShapeDtypeStruct((B, L, D), F32), jax.ShapeDtypeStruct((B, L, D), BF16)],
        scratch_shapes=[pltpu.VMEM((tm, MEM_WIDTH), BF16)],
        compiler_params=_cparams("parallel", "arbitrary"),
        name="memory_attn",
    )(xn, x, kv, wq, wo, g_post.reshape(1, D), g_next.reshape(1, D))


def _ffn_up_body(xn_ref, xp_ref, xnx_ref, wg_ref, wv_ref, cw_ref, o_ref, xe_ref, ext_ref, *, halo, tiles_per_seq):
    i = pl.program_id(0)
    tm = xn_ref.shape[0]

    @pl.when(pl.program_id(1) == 0)
    def _():
        xe_ref[0:halo, :] = xp_ref[...]
        xe_ref[halo:halo + tm, :] = xn_ref[...]
        xe_ref[halo + tm:, :] = xnx_ref[...]

    pos = lax.rem(i, tiles_per_seq)
    gate = jnp.dot(xe_ref[...], wg_ref[...], preferred_element_type=F32)
    ext_ref[0:halo, :] = jnp.where(pos == 0, 0.0, gate[0:halo])
    ext_ref[halo:halo + tm, :] = gate[halo:halo + tm]
    ext_ref[halo + tm:, :] = jnp.where(pos == tiles_per_seq - 1, 0.0, gate[halo + tm:])
    val = jnp.dot(xn_ref[...], wv_ref[...], preferred_element_type=F32)
    conv = (cw_ref[0:1, :] * ext_ref[halo - 1:halo - 1 + tm, :]
            + cw_ref[1:2, :] * ext_ref[halo:halo + tm, :]
            + cw_ref[2:3, :] * ext_ref[halo + 1:halo + 1 + tm, :])
    cdf = 0.5 * (1.0 + jnp.tanh(math.sqrt(2.0 / math.pi) * (conv + 0.044715 * (conv * conv * conv))))
    o_ref[...] = (conv * cdf * val).astype(o_ref.dtype)


def ffn_up_gated(xn, w_up, conv_w, seq_len, tm=1024, tn=512):
    T, D = xn.shape
    F = w_up.shape[1] // 2
    halo = 16
    nj = F // tn
    hb = tm // halo
    nhb = T // halo
    return pl.pallas_call(
        functools.partial(_ffn_up_body, halo=halo, tiles_per_seq=seq_len // tm),
        grid=(T // tm, nj),
        in_specs=[
            pl.BlockSpec((tm, D), lambda i, j: (i, 0)),
            pl.BlockSpec((halo, D), lambda i, j: (jnp.maximum(i * hb - 1, 0), 0)),
            pl.BlockSpec((halo, D), lambda i, j: (jnp.minimum((i + 1) * hb, nhb - 1), 0)),
            pl.BlockSpec((D, tn), lambda i, j: (0, j)),
            pl.BlockSpec((D, tn), lambda i, j: (0, nj + j)),
            pl.BlockSpec((3, tn), lambda i, j: (0, j)),
        ],
        out_specs=pl.BlockSpec((tm, tn), lambda i, j: (i, j)),
        out_shape=jax.ShapeDtypeStruct((T, F), BF16),
        scratch_shapes=[pltpu.VMEM((tm + 2 * halo, D), BF16), pltpu.VMEM((tm + 2 * halo, tn), F32)],
        compiler_params=_cparams("parallel", "arbitrary"),
        name="ffn_up",
    )(xn, xn, xn, w_up, w_up, conv_w)


def _mm_res_body(h_ref, w_ref, x_ref, gp_ref, gn_ref, xo_ref, xn_ref):
    @pl.when(pl.program_id(1) == 0)
    def _():
        _accumulate(xo_ref, h_ref[...], w_ref, first=True)

    last = pl.num_programs(1) - 1

    @pl.when((pl.program_id(1) > 0) & (pl.program_id(1) < last))
    def _():
        _accumulate(xo_ref, h_ref[...], w_ref)

    @pl.when(pl.program_id(1) == last)
    def _():
        _accumulate_and_finish(xo_ref, h_ref, w_ref, x_ref, gp_ref, gn_ref, xn_ref)


def matmul_residual(h, w, x, g_post, g_next, name, tm=512, tk=512):
    T, K = h.shape
    D = w.shape[1]
    row = lambda i, k: (i, 0)
    return pl.pallas_call(
        _mm_res_body,
        grid=(T // tm, K // tk),
        in_specs=[
            pl.BlockSpec((tm, tk), lambda i, k: (i, k)),
            pl.BlockSpec((tk, D), lambda i, k: (k, 0)),
            pl.BlockSpec((tm, D), row),
            pl.BlockSpec((1, D), lambda i, k: (0, 0)),
            pl.BlockSpec((1, D), lambda i, k: (0, 0)),
        ],
        out_specs=[pl.BlockSpec((tm, D), row), pl.BlockSpec((tm, D), row)],
        out_shape=[jax.ShapeDtypeStruct((T, D), F32), jax.ShapeDtypeStruct((T, D), BF16)],
        compiler_params=_cparams("parallel", "arbitrary"),
        name=name,
    )(h, w, x, g_post.reshape(1, D), g_next.reshape(1, D))


def _cast_pad_body(src_ref, w_ref, o_ref, *, pad_blocks):
    del src_ref
    j = pl.program_id(0)
    is_pad = functools.reduce(lambda a, b: a | b, [j == pb for pb in pad_blocks])

    @pl.when(is_pad)
    def _():
        o_ref[...] = jnp.zeros_like(o_ref)

    @pl.when(jnp.logical_not(is_pad))
    def _():
        o_ref[...] = w_ref[...].astype(o_ref.dtype)


def cast_pad_blocks(w_stack, layer, axis, blk, segments):
    _, R, C = w_stack.shape
    src_blocks, pad_blocks = [], []
    for start, length in segments:
        n = length // blk
        src_blocks += [start // blk + b for b in range(n)]
        pad_blocks.append(len(src_blocks))
        src_blocks.append(start // blk + n - 1)
    src = jnp.asarray(np.array(src_blocks, np.int32))
    nb = len(src_blocks)
    if axis == 1:
        block, out_shape = (None, R, blk), (R, nb * blk)
        in_map = lambda j, s: (layer, 0, s[j])
        out_spec = pl.BlockSpec((R, blk), lambda j, s: (0, j))
    else:
        block, out_shape = (None, blk, C), (nb * blk, C)
        in_map = lambda j, s: (layer, s[j], 0)
        out_spec = pl.BlockSpec((blk, C), lambda j, s: (j, 0))
    return pl.pallas_call(
        functools.partial(_cast_pad_body, pad_blocks=tuple(pad_blocks)),
        grid_spec=pltpu.PrefetchScalarGridSpec(
            num_scalar_prefetch=1, grid=(nb,),
            in_specs=[pl.BlockSpec(block, in_map)], out_specs=out_spec),
        out_shape=jax.ShapeDtypeStruct(out_shape, BF16),
        compiler_params=_cparams("arbitrary"),
        name="weight_cast_pad",
    )(src, w_stack)


def prepare_layer_params(p, l):
    w_in = p["w_in"][l].astype(BF16)
    zeros = lambda r, c: jnp.zeros((r, c), w_in.dtype)
    a_end = 3 * A_WIDTH + SSM_WIDTH + SSM_WIDTH
    bc0 = a_end
    dt0 = bc0 + 2 * SSM_GROUPS * SSM_STATE
    c0 = dt0 + 2 * SSM_HEADS
    a_cols = [w_in[:, s * A_WIDTH + g * A_OUT:s * A_WIDTH + (g + 1) * A_OUT]
              for g in range(len(DILATED_GROUPS)) for s in range(3)]
    w_in_p = jnp.concatenate(
        a_cols + [w_in[:, 3 * A_WIDTH:a_end], w_in[:, dt0:c0], zeros(D_MODEL, COL_CQ - COL_DT - 2 * SSM_HEADS),
                  w_in[:, c0:], w_in[:, bc0:dt0]], axis=1)
    fpad = D_FF_PAD - D_FF
    w_up_p = cast_pad_blocks(p["ffn_w_up"], l, 1, fpad, [(0, D_FF), (D_FF, D_FF)])
    w_down_p = cast_pad_blocks(p["ffn_w_down"], l, 0, fpad, [(0, D_FF)])
    lanes = lambda v: jnp.concatenate([v.reshape(-1), jnp.zeros((DT_LANES - 2 * SSM_HEADS,), F32)]).reshape(1, DT_LANES)
    conv_w = p["ssm_conv_w"][l]
    conv_b = p["ssm_conv_b"][l]
    return dict(
        w_in=w_in_p, w_up=w_up_p, w_down=w_down_p,
        w_out=p["w_out"][l].astype(BF16),
        mem_wq=p["mem_wq"][l].astype(BF16), mem_wo=p["mem_wo"][l].astype(BF16),
        mem_wkv=jnp.concatenate([p["mem_wk"][l].astype(BF16), p["mem_wv"][l].astype(BF16)], axis=1),
        ffn_conv_w=jnp.concatenate([p["ffn_conv_w"][l], jnp.zeros((3, fpad), F32)], axis=1),
        conv_w_x=conv_w[:, :SSM_WIDTH], conv_b_x=conv_b[:SSM_WIDTH],
        conv_w_bc=conv_w[:, SSM_WIDTH:], conv_b_bc=conv_b[SSM_WIDTH:],
        dt_bias=lanes(p["ssm_dt_bias"][l]), a_log=lanes(p["ssm_a_log"][l]),
        d_row=jnp.repeat(p["ssm_d"][l], SSM_HEAD_DIM).reshape(1, SSM_WIDTH),
        ssm_norm=p["ssm_norm"][l].reshape(1, SSM_WIDTH),
        na_rpb=p["na_rpb"][l],
        mix_norm_pre=p["mix_norm_pre"][l], mix_norm_post=p["mix_norm_post"][l],
        mem_norm_pre=p["mem_norm_pre"][l], mem_norm_post=p["mem_norm_post"][l], mem_norm_kv=p["mem_norm_kv"][l],
        ffn_norm_pre=p["ffn_norm_pre"][l], ffn_norm_post=p["ffn_norm_post"][l],
    )


def hybrid_mixer(xn, lp, tab, B, L):
    proj = matmul(xn, lp["w_in"], 1024, 1024, "in_proj").reshape(B, L, IN_COLS_PAD)
    outs, lses = [], []
    for g, (window, dil) in enumerate(DILATED_GROUPS):
        o, lse = dilated_attention(proj, tab, g, dil, window // (2 * dil))
        outs.append(o)
        lses.append(lse)
    o_a = combine_groups(outs, lses)
    xs = conv_silu(proj, COL_X, lp["conv_w_x"], lp["conv_b_x"])
    bc = conv_silu(proj, COL_B, lp["conv_w_bc"], lp["conv_b_bc"])
    o_b = ssd_mixer(proj, xs, bc, lp["dt_bias"], lp["a_log"], lp["d_row"], lp["ssm_norm"])
    o_c = neighbourhood_attention(proj, na_bias_table(lp["na_rpb"], L // GRID_W))
    return o_a, o_b, o_c


def encoder_layer(x, xn, mem, lp, g_next, tab):
    B, L, D = x.shape
    T = B * L
    o_a, o_b, o_c = hybrid_mixer(xn, lp, tab, B, L)
    x2, xn = mixer_out_proj(o_a, o_b, o_c, lp["w_out"], x.reshape(T, D), lp["mix_norm_post"], lp["mem_norm_pre"])
    mem_n = rmsnorm(mem.reshape(B * MEM_TOKENS, D), lp["mem_norm_kv"])
    kv = matmul(mem_n, lp["mem_wkv"], 256, 1024, "mem_kv_proj").reshape(B, MEM_TOKENS, 2 * MEM_WIDTH)
    x3, xn = memory_attention(xn.reshape(B, L, D), x2.reshape(B, L, D), kv, lp["mem_wq"], lp["mem_wo"],
                              lp["mem_norm_post"], lp["ffn_norm_pre"])
    act = ffn_up_gated(xn.reshape(T, D), lp["w_up"], lp["ffn_conv_w"], L)
    x4, xn = matmul_residual(act, lp["w_down"], x3.reshape(T, D), lp["ffn_norm_post"], g_next, "ffn_down")
    return x4.reshape(B, L, D), xn


def kernel(x_prompt, x_sample, mem_prompt, mem_sample, mix_norm_pre, mix_norm_post, w_in, ssm_conv_w, ssm_conv_b,
           ssm_a_log, ssm_dt_bias, ssm_d, ssm_norm, na_rpb, w_out, mem_norm_pre, mem_norm_post, mem_norm_kv,
           mem_wq, mem_wk, mem_wv, mem_wo, ffn_norm_pre, ffn_norm_post, ffn_w_up, ffn_conv_w, ffn_w_down):
    p = dict(mix_norm_pre=mix_norm_pre, mix_norm_post=mix_norm_post, w_in=w_in, ssm_conv_w=ssm_conv_w,
             ssm_conv_b=ssm_conv_b, ssm_a_log=ssm_a_log, ssm_dt_bias=ssm_dt_bias, ssm_d=ssm_d, ssm_norm=ssm_norm,
             na_rpb=na_rpb, w_out=w_out, mem_norm_pre=mem_norm_pre, mem_norm_post=mem_norm_post,
             mem_norm_kv=mem_norm_kv, mem_wq=mem_wq, mem_wk=mem_wk, mem_wv=mem_wv, mem_wo=mem_wo,
             ffn_norm_pre=ffn_norm_pre, ffn_norm_post=ffn_norm_post, ffn_w_up=ffn_w_up, ffn_conv_w=ffn_conv_w,
             ffn_w_down=ffn_w_down)
    groups = [(x_prompt, mem_prompt), (x_sample, mem_sample)]
    tabs = [rotary_table(x.shape[1]) for x, _ in groups]
    xs = [x for x, _ in groups]
    xns = [rmsnorm(x.reshape(-1, D_MODEL), mix_norm_pre[0]) for x in xs]
    for l in range(DEPTH):
        lp = prepare_layer_params(p, l)
        g_next = mix_norm_pre[l + 1] if l + 1 < DEPTH else jnp.ones((D_MODEL,), F32)
        for gi, (_, mem) in enumerate(groups):
            xs[gi], xns[gi] = encoder_layer(xs[gi], xns[gi], mem, lp, g_next, tabs[gi])
    return (xs[0], xs[1])
```
